```python
import jax, jax.numpy as jnp
from jax import lax
import numpy as np

D_MODEL = 1024
BATCH = 4
SEQ = 4096
DEPTH = 2
DEC_BATCH = 128
DEC_SEQ = 8
PAST_LEN = 2048
PAGE_SIZE = 128

HEAD_DIM = 64
A_GROUPS = ((128, 1), (512, 4), (2048, 16))
A_HPG = 4
A_HEADS = A_HPG * len(A_GROUPS)
A_WIDTH = A_HEADS * HEAD_DIM
A_OUT = A_HPG * HEAD_DIM
B_WIDTH = D_MODEL // 2
B_CONV = 3
C_WIDTH = D_MODEL // 2
C_CHUNK = 128
C_GROUPS = 4
C_GDIM = C_WIDTH // C_GROUPS
D_WIDTH = D_MODEL // 2
D_CONV = 31
D_FF = 2816
N_BRANCH = 4
IN_SPLITS = (A_WIDTH, A_WIDTH, A_WIDTH, B_WIDTH, B_WIDTH, B_WIDTH, C_WIDTH, C_WIDTH, 2 * D_WIDTH, N_BRANCH * D_MODEL)
IN_COLS = sum(IN_SPLITS)
EPS = 1e-6

kernel_name = "hybrid_dilated_attn_conv_gmlp_step"


def _rmsnorm(x, g):
    xf = x.astype(jnp.float32)
    y = xf * lax.rsqrt(jnp.mean(xf * xf, axis=-1, keepdims=True) + EPS)
    return (y * g.astype(jnp.float32)).astype(x.dtype)


def _layernorm(x, g, b):
    xf = x.astype(jnp.float32)
    mu = jnp.mean(xf, axis=-1, keepdims=True)
    xc = xf - mu
    var = jnp.mean(xc * xc, axis=-1, keepdims=True)
    return (xc * lax.rsqrt(var + EPS) * g.astype(jnp.float32) + b.astype(jnp.float32)).astype(x.dtype)


def _swiglu(x, wg, wu, wd):
    return (jax.nn.silu(x @ wg) * (x @ wu)) @ wd


def _half_ffn(x, pre_g, post_g, wg, wu, wd):
    return x + 0.5 * _rmsnorm(_swiglu(_rmsnorm(x, pre_g), wg, wu, wd), post_g)


def _alibi_slopes():
    return jnp.exp2(-8.0 * jnp.arange(1, A_HEADS + 1, dtype=jnp.float32) / A_HEADS)


def _dwconv(xpad, w):
    c = xpad.shape[-1]
    return lax.conv_general_dilated(xpad, w[:, None, :], window_strides=(1,), padding='VALID',
                                    dimension_numbers=('NWC', 'WIO', 'NWC'), feature_group_count=c)


def _softmax_with_lse(s):
    m = jnp.max(s, axis=-1, keepdims=True)
    p = jnp.exp(s - m)
    den = jnp.sum(p, axis=-1, keepdims=True)
    return p / den, (m + jnp.log(den))[..., 0]


def _dilated_prompt(q, k, v, window, dil, slopes):
    n, s_len, h, dh = q.shape
    nk = window // dil
    ln = s_len // dil
    nblk = -(-ln // nk)
    lp = nblk * nk

    def to_blocks(t):
        t = t.reshape(n, ln, dil, h, dh).transpose(0, 2, 1, 3, 4)
        t = jnp.pad(t, ((0, 0), (0, 0), (0, lp - ln), (0, 0), (0, 0)))
        return t.reshape(n, dil, nblk, nk, h, dh)

    def with_prev(t):
        prev = jnp.pad(t, ((0, 0), (0, 0), (1, 0), (0, 0), (0, 0), (0, 0)))[:, :, :-1]
        return jnp.concatenate([prev, t], axis=3)

    qb = to_blocks(q)
    kk = with_prev(to_blocks(k))
    vv = with_prev(to_blocks(v))
    s = jnp.einsum('brnqhd,brnkhd->brnhqk', qb, kk).astype(jnp.float32) * (dh ** -0.5)
    rel = nk + jnp.arange(nk)[:, None] - jnp.arange(2 * nk)[None, :]
    kidx = jnp.arange(nblk)[:, None] * nk - nk + jnp.arange(2 * nk)[None, :]
    valid = ((rel >= 0) & (rel <= nk))[None] & (kidx >= 0)[:, None, :]
    bias = -slopes[:, None, None] * (dil * rel).astype(jnp.float32)
    s = jnp.where(valid[:, None], s + bias, -jnp.inf)
    p, lse = _softmax_with_lse(s)
    o = jnp.einsum('brnhqk,brnkhd->brnqhd', p, vv.astype(jnp.float32))
    o = o.reshape(n, dil, lp, h, dh)[:, :, :ln].transpose(0, 2, 1, 3, 4).reshape(n, s_len, h, dh)
    lse = lse.transpose(0, 1, 2, 4, 3).reshape(n, dil, lp, h)[:, :, :ln].transpose(0, 2, 1, 3).reshape(n, s_len, h)
    return o, lse


def _dilated_sample(q, k_new, v_new, k_buf, v_buf, window, dil, slopes):
    n, t_len, h, dh = q.shape
    nk = window // dil
    lw = k_buf.shape[1]
    kk = jnp.concatenate([k_buf, k_new], axis=1)
    vv = jnp.concatenate([v_buf, v_new], axis=1)
    steps = jnp.arange(nk + 1)
    idx = lw + jnp.arange(t_len)[:, None] - dil * steps[None, :]
    valid = idx >= 0
    idx = jnp.maximum(idx, 0)
    kg = kk[:, idx]
    vg = vv[:, idx]
    s = jnp.einsum('bthd,btkhd->bhtk', q, kg).astype(jnp.float32) * (dh ** -0.5)
    bias = -slopes[:, None, None] * (dil * steps).astype(jnp.float32)[None, None, :]
    s = jnp.where(valid[None, None], s + bias, -jnp.inf)
    p, lse = _softmax_with_lse(s)
    o = jnp.einsum('bhtk,btkhd->bthd', p, vg.astype(jnp.float32))
    return o, lse.transpose(0, 2, 1)


def _mixer(h, W, l, st, prompt):
    n, t_len, _ = h.shape
    cuts = [int(c) for c in np.cumsum(IN_SPLITS)[:-1]]
    z = h @ W['w_in'][l]
    q, k, v, b_gate, c_gate, b_in, u, gv, glu_in, gates = jnp.split(z, cuts, axis=-1)

    q = q.reshape(n, t_len, A_HEADS, HEAD_DIM)
    k = k.reshape(n, t_len, A_HEADS, HEAD_DIM)
    v = v.reshape(n, t_len, A_HEADS, HEAD_DIM)
    slopes = _alibi_slopes()
    outs, lses, new_attn = [], [], []
    for g, (win, dil) in enumerate(A_GROUPS):
        hs = slice(g * A_HPG, (g + 1) * A_HPG)
        qg, kg, vg = q[:, :, hs], k[:, :, hs], v[:, :, hs]
        if prompt:
            o, lse = _dilated_prompt(qg, kg, vg, win, dil, slopes[hs])
            keep = min(win, t_len)
            new_attn.append(jnp.stack([kg[:, t_len - keep:], vg[:, t_len - keep:]], axis=2))
        else:
            buf = st[0][g]
            o, lse = _dilated_sample(qg, kg, vg, buf[:, :, 0], buf[:, :, 1], win, dil, slopes[hs])
            new_attn.append(jnp.stack([kg, vg], axis=2))
        outs.append(o)
        lses.append(lse)
    wts = jax.nn.softmax(jnp.stack(lses, axis=0), axis=0)
    attn = jnp.sum(wts[..., None] * jnp.stack(outs, axis=0), axis=0)
    br_a = attn.reshape(n, t_len, A_OUT).astype(h.dtype) @ W['w_out_a'][l]

    zc = c_gate * b_in
    prev_b = jnp.zeros((n, B_CONV - 1, B_WIDTH), h.dtype) if prompt else st[1]
    zpad = jnp.concatenate([prev_b, zc], axis=1)
    br_b = (b_gate * _dwconv(zpad, W['conv_b_w'][l])) @ W['w_out_b'][l]
    new_b = zpad[:, -(B_CONV - 1):]

    vn = _layernorm(gv, W['gmlp_ln_g'][l], W['gmlp_ln_b'][l])
    ws = jnp.where(jnp.tril(jnp.ones((C_CHUNK, C_CHUNK), dtype=bool)), W['gmlp_ws'][l], 0)
    if prompt:
        vr = vn.reshape(n, t_len // C_CHUNK, C_CHUNK, C_GROUPS, C_GDIM)
        mixed = jnp.einsum('gts,bnsgc->bntgc', ws, vr) + W['gmlp_b'][l].T[None, None, :, :, None]
    else:
        vr = vn.reshape(n, t_len, C_GROUPS, C_GDIM)
        mixed = jnp.einsum('gts,bsgc->btgc', ws[:, :t_len, :t_len], vr) + W['gmlp_b'][l][:, :t_len].T[None, :, :, None]
    br_c = (u * mixed.reshape(n, t_len, C_WIDTH)) @ W['w_out_c'][l]

    a_half, g_half = jnp.split(glu_in, 2, axis=-1)
    glu = a_half * jax.nn.sigmoid(g_half)
    prev_d = jnp.zeros((n, D_CONV - 1, D_WIDTH), h.dtype) if prompt else st[2]
    gpad = jnp.concatenate([prev_d, glu], axis=1)
    dc = _dwconv(gpad, W['conv_d_w'][l]) + W['conv_d_b'][l]
    br_d = jax.nn.silu(_layernorm(dc, W['conv_d_ln_g'][l], W['conv_d_ln_b'][l])) @ W['w_out_d'][l]
    new_d = gpad[:, -(D_CONV - 1):]

    gs = jax.nn.sigmoid(gates).reshape(n, t_len, N_BRANCH, D_MODEL)
    merged = gs[:, :, 0] * br_a + gs[:, :, 1] * br_b + gs[:, :, 2] * br_c + gs[:, :, 3] * br_d
    y = merged @ W['w_o'][l]
    return y, new_attn, new_b, vn, new_d


def _layer(x, W, l, st, prompt):
    x = _half_ffn(x, W['ffn1_pre_g'][l], W['ffn1_post_g'][l], W['ffn1_w_gate'][l], W['ffn1_w_up'][l], W['ffn1_w_down'][l])
    y, new_attn, new_b, vn, new_d = _mixer(_rmsnorm(x, W['mix_pre_g'][l]), W, l, st, prompt)
    x = x + _rmsnorm(y, W['mix_post_g'][l])
    x = _half_ffn(x, W['ffn2_pre_g'][l], W['ffn2_post_g'][l], W['ffn2_w_gate'][l], W['ffn2_w_up'][l], W['ffn2_w_down'][l])
    return x, new_attn, new_b, vn, new_d


def setup_inputs(seed: int = 0) -> dict:
    key = jax.random.key(seed)
    keys = list(jax.random.split(key, 64))

    def nrm(shape, scale):
        return jax.random.normal(keys.pop(), shape, jnp.float32) * scale

    def gain(width):
        return 1.0 + nrm((DEPTH, width), 0.02)

    inp = {}
    inp['x_prompt'] = nrm((BATCH, SEQ, D_MODEL), 1.0)
    inp['x_sample'] = nrm((DEC_BATCH, DEC_SEQ, D_MODEL), 1.0)
    for win, _ in A_GROUPS:
        inp['cache_attn_w' + str(win)] = nrm((DEPTH, DEC_BATCH, min(win, PAST_LEN), 2, A_HPG, HEAD_DIM), 1.0)
    inp['state_conv_b'] = nrm((DEPTH, DEC_BATCH, B_CONV - 1, B_WIDTH), 1.0)
    inp['state_conv_d'] = nrm((DEPTH, DEC_BATCH, D_CONV - 1, D_WIDTH), 0.5)
    inp['ffn1_pre_g'] = gain(D_MODEL)
    inp['ffn1_post_g'] = gain(D_MODEL)
    inp['ffn1_w_gate'] = nrm((DEPTH, D_MODEL, D_FF), D_MODEL ** -0.5)
    inp['ffn1_w_up'] = nrm((DEPTH, D_MODEL, D_FF), D_MODEL ** -0.5)
    inp['ffn1_w_down'] = nrm((DEPTH, D_FF, D_MODEL), D_FF ** -0.5)
    inp['mix_pre_g'] = gain(D_MODEL)
    inp['mix_post_g'] = gain(D_MODEL)
    inp['w_in'] = nrm((DEPTH, D_MODEL, IN_COLS), D_MODEL ** -0.5)
    inp['w_out_a'] = nrm((DEPTH, A_OUT, D_MODEL), A_OUT ** -0.5)
    inp['conv_b_w'] = nrm((DEPTH, B_CONV, B_WIDTH), B_CONV ** -0.5)
    inp['w_out_b'] = nrm((DEPTH, B_WIDTH, D_MODEL), B_WIDTH ** -0.5)
    inp['gmlp_ln_g'] = gain(C_WIDTH)
    inp['gmlp_ln_b'] = nrm((DEPTH, C_WIDTH), 0.02)
    inp['gmlp_ws'] = nrm((DEPTH, C_GROUPS, C_CHUNK, C_CHUNK), C_CHUNK ** -0.5)
    inp['gmlp_b'] = 1.0 + nrm((DEPTH, C_GROUPS, C_CHUNK), 0.05)
    inp['w_out_c'] = nrm((DEPTH, C_WIDTH, D_MODEL), C_WIDTH ** -0.5)
    inp['conv_d_w'] = nrm((DEPTH, D_CONV, D_WIDTH), D_CONV ** -0.5)
    inp['conv_d_b'] = nrm((DEPTH, D_WIDTH), 0.02)
    inp['conv_d_ln_g'] = gain(D_WIDTH)
    inp['conv_d_ln_b'] = nrm((DEPTH, D_WIDTH), 0.02)
    inp['w_out_d'] = nrm((DEPTH, D_WIDTH, D_MODEL), D_WIDTH ** -0.5)
    inp['w_o'] = nrm((DEPTH, D_MODEL, D_MODEL), D_MODEL ** -0.5)
    inp['ffn2_pre_g'] = gain(D_MODEL)
    inp['ffn2_post_g'] = gain(D_MODEL)
    inp['ffn2_w_gate'] = nrm((DEPTH, D_MODEL, D_FF), D_MODEL ** -0.5)
    inp['ffn2_w_up'] = nrm((DEPTH, D_MODEL, D_FF), D_MODEL ** -0.5)
    inp['ffn2_w_down'] = nrm((DEPTH, D_FF, D_MODEL), D_FF ** -0.5)
    return inp


def reference(x_prompt, x_sample, cache_attn_w128, cache_attn_w512, cache_attn_w2048, state_conv_b, state_conv_d,
              ffn1_pre_g, ffn1_post_g, ffn1_w_gate, ffn1_w_up, ffn1_w_down, mix_pre_g, mix_post_g, w_in,
              w_out_a, conv_b_w, w_out_b, gmlp_ln_g, gmlp_ln_b, gmlp_ws, gmlp_b, w_out_c, conv_d_w, conv_d_b,
              conv_d_ln_g, conv_d_ln_b, w_out_d, w_o, ffn2_pre_g, ffn2_post_g, ffn2_w_gate, ffn2_w_up, ffn2_w_down):
    W = dict(ffn1_pre_g=ffn1_pre_g, ffn1_post_g=ffn1_post_g, ffn1_w_gate=ffn1_w_gate, ffn1_w_up=ffn1_w_up,
             ffn1_w_down=ffn1_w_down, mix_pre_g=mix_pre_g, mix_post_g=mix_post_g, w_in=w_in, w_out_a=w_out_a,
             conv_b_w=conv_b_w, w_out_b=w_out_b, gmlp_ln_g=gmlp_ln_g, gmlp_ln_b=gmlp_ln_b, gmlp_ws=gmlp_ws,
             gmlp_b=gmlp_b, w_out_c=w_out_c, conv_d_w=conv_d_w, conv_d_b=conv_d_b, conv_d_ln_g=conv_d_ln_g,
             conv_d_ln_b=conv_d_ln_b, w_out_d=w_out_d, w_o=w_o, ffn2_pre_g=ffn2_pre_g, ffn2_post_g=ffn2_post_g,
             ffn2_w_gate=ffn2_w_gate, ffn2_w_up=ffn2_w_up, ffn2_w_down=ffn2_w_down)
    attn_caches = (cache_attn_w128, cache_attn_w512, cache_attn_w2048)
    yp, ys = x_prompt, x_sample
    pa = [[] for _ in A_GROUPS]
    sa = [[] for _ in A_GROUPS]
    pb, sb, sc, pd, sd = [], [], [], [], []
    for l in range(DEPTH):
        yp, na, nb, _, nd = _layer(yp, W, l, None, True)
        for g in range(len(A_GROUPS)):
            pa[g].append(na[g])
        pb.append(nb)
        pd.append(nd)
        st = ([c[l] for c in attn_caches], state_conv_b[l], state_conv_d[l])
        ys, na, nb, vn, nd = _layer(ys, W, l, st, False)
        for g in range(len(A_GROUPS)):
            sa[g].append(na[g])
        sb.append(nb)
        sc.append(vn)
        sd.append(nd)
    new_attn_w128_prompt = jnp.stack(pa[0], axis=0)
    new_attn_w512_prompt = jnp.stack(pa[1], axis=0)
    new_attn_w2048_prompt = jnp.stack(pa[2], axis=0)
    new_attn_w128_sample = jnp.stack(sa[0], axis=0)
    new_attn_w512_sample = jnp.stack(sa[1], axis=0)
    new_attn_w2048_sample = jnp.stack(sa[2], axis=0)
    new_conv_b_prompt = jnp.stack(pb, axis=0)
    new_conv_b_sample = jnp.stack(sb, axis=0)
    new_gmlp_v_sample = jnp.stack(sc, axis=0)
    new_conv_d_prompt = jnp.stack(pd, axis=0)
    new_conv_d_sample = jnp.stack(sd, axis=0)
    return (yp, ys, new_attn_w128_prompt, new_attn_w512_prompt, new_attn_w2048_prompt,
            new_attn_w128_sample, new_attn_w512_sample, new_attn_w2048_sample,
            new_conv_b_prompt, new_conv_b_sample, new_gmlp_v_sample, new_conv_d_prompt, new_conv_d_sample)
```

```python
import functools

import numpy as np
import jax
import jax.numpy as jnp
from jax import lax
from jax.experimental import pallas as pl
from jax.experimental.pallas import tpu as pltpu

D_MODEL = 1024
BATCH = 4
SEQ = 4096
DEPTH = 2
DEC_BATCH = 128
DEC_SEQ = 8
HEAD_DIM = 64
A_GROUPS = ((128, 1), (512, 4), (2048, 16))
A_HPG = 4
A_HEADS = A_HPG * len(A_GROUPS)
A_WIDTH = A_HEADS * HEAD_DIM
A_OUT = A_HPG * HEAD_DIM
NK = 128
B_WIDTH = 512
B_CONV = 3
C_WIDTH = 512
C_CHUNK = 128
C_GROUPS = 4
C_GDIM = 128
D_WIDTH = 512
D_CONV = 31
D_FF = 2816
EPS = 1e-6
QKV_COLS = 3 * A_WIDTH
OFF_BGATE, OFF_CGATE, OFF_BIN, OFF_U, OFF_GV, OFF_GLUA, OFF_GLUG, OFF_GATES = (
    0, 512, 1024, 1536, 2048, 2560, 3072, 3584)
REST_COLS = OFF_GATES + 4 * D_MODEL

F32 = jnp.float32
BF16 = jnp.bfloat16
NEG = -1e30
V7X_VMEM_LIMIT_BYTES = 56 * 1024 * 1024

FFN_TM = 512
FFN_FC = 256
QKV_TM = 512
MIX_TM = 256
MIX_RC = 64
ZHALO = 8
GHALO = 32
ATT_TQ = 512
SAMP_SB = 4
SAMP_SN = 32


def _cparams(*sem):
    return pltpu.CompilerParams(dimension_semantics=sem, vmem_limit_bytes=V7X_VMEM_LIMIT_BYTES)


def _resident(shape):
    nd = len(shape)
    return pl.BlockSpec(shape, lambda *_: (0,) * nd, pipeline_mode=pl.Buffered(1))


def _dot(a, b):
    return jnp.dot(a, b, preferred_element_type=F32)


def _dot_nt(a, b):
    return lax.dot_general(a, b, (((1,), (1,)), ((), ())), preferred_element_type=F32)


def _rms(x, g):
    return x * lax.rsqrt(jnp.mean(x * x, axis=-1, keepdims=True) + EPS) * g


def _ln(x, g, b):
    mu = jnp.mean(x, axis=-1, keepdims=True)
    xc = x - mu
    var = jnp.mean(xc * xc, axis=-1, keepdims=True)
    return xc * lax.rsqrt(var + EPS) * g + b


def _ffn_body(x_ref, pre_ref, post_ref, wg_ref, wu_ref, wd_ref, o_ref):
    x = x_ref[...]
    h = _rms(x, pre_ref[...]).astype(BF16)
    acc = jnp.zeros(x.shape, F32)
    for c in range(D_FF // FFN_FC):
        sl = slice(c * FFN_FC, (c + 1) * FFN_FC)
        g = _dot(h, wg_ref[:, sl])
        u = _dot(h, wu_ref[:, sl])
        acc = acc + _dot((jax.nn.silu(g) * u).astype(BF16), wd_ref[sl, :])
    o_ref[...] = x + 0.5 * _rms(acc, post_ref[...])


def _ffn(x, pre_g, post_g, wg, wu, wd):
    rows = x.shape[0]
    tm = min(FFN_TM, rows)
    return pl.pallas_call(
        _ffn_body,
        grid=(rows // tm,),
        in_specs=[pl.BlockSpec((tm, D_MODEL), lambda i: (i, 0)),
                  _resident((1, D_MODEL)), _resident((1, D_MODEL)),
                  _resident((D_MODEL, D_FF)), _resident((D_MODEL, D_FF)), _resident((D_FF, D_MODEL))],
        out_specs=pl.BlockSpec((tm, D_MODEL), lambda i: (i, 0)),
        out_shape=jax.ShapeDtypeStruct((rows, D_MODEL), F32),
        compiler_params=_cparams("parallel"),
        name="ffn",
    )(x, pre_g, post_g, wg, wu, wd)


def _qkv_body(x_ref, g_ref, w_ref, q_ref, k_ref, v_ref):
    h = _rms(x_ref[...], g_ref[...]).astype(BF16)
    q_ref[...] = (_dot(h, w_ref[:, 0:A_WIDTH]) * (HEAD_DIM ** -0.5)).astype(BF16)
    k_ref[...] = _dot(h, w_ref[:, A_WIDTH:2 * A_WIDTH]).astype(BF16)
    v_ref[...] = _dot(h, w_ref[:, 2 * A_WIDTH:3 * A_WIDTH]).astype(BF16)


def _qkv(x, g, w_qkv):
    rows = x.shape[0]
    tm = min(QKV_TM, rows)
    out = jax.ShapeDtypeStruct((rows, A_WIDTH), BF16)
    spec = pl.BlockSpec((tm, A_WIDTH), lambda i: (i, 0))
    return pl.pallas_call(
        _qkv_body,
        grid=(rows // tm,),
        in_specs=[pl.BlockSpec((tm, D_MODEL), lambda i: (i, 0)),
                  _resident((1, D_MODEL)), _resident((D_MODEL, QKV_COLS))],
        out_specs=[spec, spec, spec],
        out_shape=[out, out, out],
        compiler_params=_cparams("parallel"),
        name="qkv",
    )(x, g, w_qkv)


def _alibi_slopes():
    return np.exp2(-8.0 * np.arange(1, A_HEADS + 1, dtype=np.float64) / A_HEADS)


def _prompt_bias(group):
    _, dil = A_GROUPS[group]
    rel = NK + np.arange(NK)[:, None] - np.arange(2 * NK)[None, :]
    valid = (rel >= 0) & (rel <= NK)
    slopes = _alibi_slopes()[group * A_HPG:(group + 1) * A_HPG]
    bias = -slopes[:, None, None] * (dil * rel)[None].astype(np.float64)
    return np.where(valid[None], bias, NEG).astype(np.float32)


def _attn_p_body(q_ref, kp_ref, kc_ref, vp_ref, vc_ref, bias_ref, o_ref, lse_ref, *, tq):
    first = (pl.program_id(2) == 0).astype(F32)
    lane_head = lax.broadcasted_iota(jnp.int32, (1, A_OUT), 1) // HEAD_DIM
    key_col = lax.broadcasted_iota(jnp.int32, (1, 2 * NK), 1)
    no_prev = jnp.where(key_col < NK, NEG, 0.0) * first
    for j in range(tq // NK):
        rows = slice(j * NK, (j + 1) * NK)
        qj = q_ref[0, rows, :]
        if j == 0:
            k_prev, v_prev = kp_ref[0], vp_ref[0]
        else:
            prev = slice((j - 1) * NK, j * NK)
            k_prev, v_prev = kc_ref[0, prev, :], vc_ref[0, prev, :]
        k2 = jnp.concatenate([k_prev, kc_ref[0, rows, :]], axis=0)
        v2 = jnp.concatenate([v_prev, vc_ref[0, rows, :]], axis=0)
        o_acc = jnp.zeros((NK, A_OUT), F32)
        l_acc = jnp.zeros((NK, A_OUT), F32)
        for h in range(A_HPG):
            sel = lane_head == h
            qh = jnp.where(sel, qj, jnp.zeros_like(qj))
            s = _dot_nt(qh, k2) + bias_ref[h]
            if j == 0:
                s = s + no_prev
            m = jnp.max(s, axis=-1, keepdims=True)
            p = jnp.exp(s - m)
            den = jnp.sum(p, axis=-1, keepdims=True)
            o_h = _dot(p.astype(BF16), v2) / den
            o_acc = jnp.where(sel, o_h, o_acc)
            l_acc = jnp.where(sel, m + jnp.log(den), l_acc)
        o_ref[0, rows, :] = o_acc
        lse_ref[0, rows, :] = l_acc


def _attn_prompt(q, k, v, group):
    _, dil = A_GROUPS[group]
    sub = SEQ // dil
    tq = min(ATT_TQ, sub)
    nsteps = sub // tq
    qr = q.reshape(BATCH, sub, dil * A_WIDTH)
    kr = k.reshape(BATCH, sub, dil * A_WIDTH)
    vr = v.reshape(BATCH, sub, dil * A_WIDTH)
    ncol = A_WIDTH // A_OUT
    cur = pl.BlockSpec((1, tq, A_OUT), lambda b, r, i: (b, i, r * ncol + group))
    prev = pl.BlockSpec((1, NK, A_OUT),
                        lambda b, r, i: (b, jnp.maximum(i * (tq // NK) - 1, 0), r * ncol + group))
    out_spec = pl.BlockSpec((1, tq, A_OUT), lambda b, r, i: (b, i, r))
    out = jax.ShapeDtypeStruct((BATCH, sub, dil * A_OUT), F32)
    o, lse = pl.pallas_call(
        functools.partial(_attn_p_body, tq=tq),
        grid=(BATCH, dil, nsteps),
        in_specs=[cur, prev, cur, prev, cur, _resident((A_HPG, NK, 2 * NK))],
        out_specs=[out_spec, out_spec],
        out_shape=[out, out],
        compiler_params=_cparams("parallel", "parallel", "arbitrary"),
        name=f"attn_prompt_g{group}",
    )(qr, kr, kr, vr, vr, jnp.asarray(_prompt_bias(group)))
    return o.reshape(BATCH * SEQ, A_OUT), lse.reshape(BATCH * SEQ, A_OUT)


SAMP_CACHE_ROWS = (128, 512, 1024)
SAMP_NEW_SLOT = 128
SAMP_SEG = []
_off = 0
for _rows in SAMP_CACHE_ROWS:
    SAMP_SEG.append((_off, _off + _rows, _off + _rows + SAMP_NEW_SLOT))
    _off += _rows + SAMP_NEW_SLOT
SAMP_KEYS = _off
SAMP_Q = A_HPG * DEC_SEQ


def _sample_bias():
    slopes = _alibi_slopes()
    bias = np.full((SAMP_Q, SAMP_KEYS), NEG, np.float64)
    for c in range(SAMP_Q):
        h, t = divmod(c, DEC_SEQ)
        for g, (win, dil) in enumerate(A_GROUPS):
            c0, n0, _ = SAMP_SEG[g]
            sl = slopes[g * A_HPG + h]
            if g < 2:
                r = np.arange(win)
                dist = win + t - r
                ok = (dist % dil == 0) & (dist <= win)
                bias[c, c0:c0 + win] = np.where(ok, -sl * dist, NEG)
            else:
                m = np.arange(NK)
                bias[c, c0 + t * NK:c0 + (t + 1) * NK] = -sl * dil * (NK - m)
            for t2 in range(t + 1):
                if (t - t2) % dil == 0:
                    bias[c, n0 + t2] = -sl * (t - t2)
    return bias.astype(np.float32)


def _attn_s_body(q_ref, k_ref, v_ref, c1_ref, c2_ref, c3_ref, bias_ref, o_ref, kall, vall, *, sb):
    @pl.when(pl.program_id(0) == 0)
    def _():
        kall[...] = jnp.zeros(kall.shape, BF16)
        vall[...] = jnp.zeros(vall.shape, BF16)

    lane_head = lax.broadcasted_iota(jnp.int32, (1, A_OUT), 1) // HEAD_DIM
    pad = jnp.zeros((16 - DEC_SEQ, A_OUT), F32)

    def one_seq(n, carry):
        qn = q_ref[n].astype(F32)
        kn = k_ref[n].astype(F32)
        vn = v_ref[n].astype(F32)
        caches = (c1_ref[0, n], c2_ref[0, n])
        for g in range(3):
            c0, n0, _ = SAMP_SEG[g]
            cols = slice(g * A_OUT, (g + 1) * A_OUT)
            if g < 2:
                cache = caches[g]
                kall[c0:n0, :] = cache[:, :A_OUT].astype(BF16)
                vall[c0:n0, :] = cache[:, A_OUT:].astype(BF16)
            else:
                for t in range(DEC_SEQ):
                    base = t * 2 * A_OUT
                    rows = slice(c0 + t * NK, c0 + (t + 1) * NK)
                    kall[rows, :] = c3_ref[0, n, :, base:base + A_OUT].astype(BF16)
                    vall[rows, :] = c3_ref[0, n, :, base + A_OUT:base + 2 * A_OUT].astype(BF16)
            kall[n0:n0 + 16, :] = jnp.concatenate([kn[:, cols], pad], axis=0).astype(BF16)
            vall[n0:n0 + 16, :] = jnp.concatenate([vn[:, cols], pad], axis=0).astype(BF16)
        scores = []
        for g in range(3):
            c0, _, e0 = SAMP_SEG[g]
            qg = qn[:, g * A_OUT:(g + 1) * A_OUT]
            qbd = jnp.concatenate([jnp.where(lane_head == h, qg, 0.0) for h in range(A_HPG)],
                                  axis=0).astype(BF16)
            scores.append(_dot_nt(qbd, kall[c0:e0, :]) + bias_ref[:, c0:e0])
        m = functools.reduce(jnp.maximum, [jnp.max(s, axis=-1, keepdims=True) for s in scores])
        den = jnp.zeros((SAMP_Q, 1), F32)
        out = jnp.zeros((SAMP_Q, A_OUT), F32)
        for g in range(3):
            c0, _, e0 = SAMP_SEG[g]
            p = jnp.exp(scores[g] - m)
            den = den + jnp.sum(p, axis=-1, keepdims=True)
            out = out + _dot(p.astype(BF16), vall[c0:e0, :])
        out = out / den
        attn = jnp.zeros((DEC_SEQ, A_OUT), F32)
        for h in range(A_HPG):
            attn = jnp.where(lane_head == h, out[h * DEC_SEQ:(h + 1) * DEC_SEQ, :], attn)
        o_ref[n] = attn
        return carry

    lax.fori_loop(0, sb, one_seq, 0)


def _attn_sample(q, k, v, c1, c2, c3, layer):
    sb = SAMP_SB
    qspec = pl.BlockSpec((sb, DEC_SEQ, A_WIDTH), lambda i: (i, 0, 0))
    return pl.pallas_call(
        functools.partial(_attn_s_body, sb=sb),
        grid=(DEC_BATCH // sb,),
        in_specs=[qspec, qspec, qspec,
                  pl.BlockSpec((1, sb, 128, 2 * A_OUT), lambda i: (layer, i, 0, 0)),
                  pl.BlockSpec((1, sb, 512, 2 * A_OUT), lambda i: (layer, i, 0, 0)),
                  pl.BlockSpec((1, sb, NK, DEC_SEQ * 2 * A_OUT), lambda i: (layer, i, 0, 0)),
                  _resident((SAMP_Q, SAMP_KEYS))],
        out_specs=pl.BlockSpec((sb, DEC_SEQ, A_OUT), lambda i: (i, 0, 0)),
        out_shape=jax.ShapeDtypeStruct((DEC_BATCH, DEC_SEQ, A_OUT), F32),
        scratch_shapes=[pltpu.VMEM((SAMP_KEYS, A_OUT), BF16), pltpu.VMEM((SAMP_KEYS, A_OUT), BF16)],
        compiler_params=_cparams("arbitrary"),
        name="attn_sample",
    )(q, k, v, c1, c2, c3, jnp.asarray(_sample_bias()))


def _proj(h, w_ref, off, width):
    return _dot(h, w_ref[:, off:off + width])


def _gate(h, w_ref, branch):
    return jax.nn.sigmoid(_proj(h, w_ref, OFF_GATES + branch * D_MODEL, D_MODEL))


_MIX_WEIGHT_SHAPES = (
    (1, D_MODEL), (1, D_MODEL),
    (D_MODEL, REST_COLS),
    (A_OUT, D_MODEL), (B_WIDTH, D_MODEL), (C_WIDTH, D_MODEL), (D_WIDTH, D_MODEL), (D_MODEL, D_MODEL),
    (B_CONV, B_WIDTH), (1, C_WIDTH), (1, C_WIDTH),
    (D_CONV, D_WIDTH), (1, D_WIDTH), (1, D_WIDTH), (1, D_WIDTH),
)


def _mixer_p_body(x_ref, o1_ref, o2_ref, o3_ref, l1_ref, l2_ref, l3_ref,
                  pre_ref, post_ref, w_ref, woa_ref, wob_ref, woc_ref, wod_ref, wo_ref,
                  cbw_ref, glng_ref, glnb_ref, cdw_ref, cdb_ref, dlng_ref, dlnb_ref,
                  gws_ref, gbc_ref,
                  xo_ref, nb_ref, nd_ref, zbuf, gbuf, dbuf, *, tm):
    @pl.when(pl.program_id(1) == 0)
    def _():
        zbuf[0:ZHALO, :] = jnp.zeros((ZHALO, B_WIDTH), F32)
        gbuf[0:GHALO, :] = jnp.zeros((GHALO, D_WIDTH), F32)

    x = x_ref[...]
    h = _rms(x, pre_ref[...]).astype(BF16)

    l1, l2, l3 = l1_ref[...], l2_ref[...], l3_ref[...]
    mx = jnp.maximum(jnp.maximum(l1, l2), l3)
    e1, e2, e3 = jnp.exp(l1 - mx), jnp.exp(l2 - mx), jnp.exp(l3 - mx)
    attn = (e1 * o1_ref[...] + e2 * o2_ref[...] + e3 * o3_ref[...]) / (e1 + e2 + e3)
    merged = _gate(h, w_ref, 0) * _dot(attn.astype(BF16), woa_ref[...])

    zbuf[ZHALO:ZHALO + tm, :] = _proj(h, w_ref, OFF_CGATE, B_WIDTH) * _proj(h, w_ref, OFF_BIN, B_WIDTH)
    conv = jnp.zeros((tm, B_WIDTH), F32)
    for k in range(B_CONV):
        start = ZHALO - (B_CONV - 1) + k
        conv = conv + cbw_ref[k:k + 1, :] * zbuf[start:start + tm, :]
    br = _dot((_proj(h, w_ref, OFF_BGATE, B_WIDTH) * conv).astype(BF16), wob_ref[...])
    merged = merged + _gate(h, w_ref, 1) * br
    tail = zbuf[tm:tm + ZHALO, :]
    nb_ref[0] = tail
    zbuf[0:ZHALO, :] = tail

    vn = _ln(_proj(h, w_ref, OFF_GV, C_WIDTH), glng_ref[...], glnb_ref[...]).astype(BF16)
    r_i = lax.broadcasted_iota(jnp.int32, (C_CHUNK, C_CHUNK), 0)
    c_i = lax.broadcasted_iota(jnp.int32, (C_CHUNK, C_CHUNK), 1)
    ws = [jnp.where(r_i >= c_i, gws_ref[g], 0.0).astype(BF16) for g in range(C_GROUPS)]
    chunks = []
    for c in range(tm // C_CHUNK):
        rows = slice(c * C_CHUNK, (c + 1) * C_CHUNK)
        chunks.append(jnp.concatenate(
            [_dot(ws[g], vn[rows, g * C_GDIM:(g + 1) * C_GDIM]) + gbc_ref[:, g:g + 1]
             for g in range(C_GROUPS)], axis=1))
    mixed = jnp.concatenate(chunks, axis=0)
    br = _dot((_proj(h, w_ref, OFF_U, C_WIDTH) * mixed).astype(BF16), woc_ref[...])
    merged = merged + _gate(h, w_ref, 2) * br

    gbuf[GHALO:GHALO + tm, :] = (_proj(h, w_ref, OFF_GLUA, D_WIDTH)
                                 * jax.nn.sigmoid(_proj(h, w_ref, OFF_GLUG, D_WIDTH)))

    def conv_chunk(ci, carry):
        r0 = pl.multiple_of(ci * MIX_RC, MIX_RC)
        win = gbuf[pl.ds(r0, MIX_RC + GHALO), :]
        acc = jnp.broadcast_to(cdb_ref[...], (MIX_RC, D_WIDTH))
        for k in range(D_CONV):
            start = GHALO - (D_CONV - 1) + k
            acc = acc + cdw_ref[k:k + 1, :] * win[start:start + MIX_RC, :]
        dbuf[pl.ds(r0, MIX_RC), :] = acc
        return carry

    lax.fori_loop(0, tm // MIX_RC, conv_chunk, 0)
    tail = gbuf[tm:tm + GHALO, :]
    nd_ref[0] = tail
    gbuf[0:GHALO, :] = tail
    dn = _ln(dbuf[...], dlng_ref[...], dlnb_ref[...])
    br = _dot(jax.nn.silu(dn).astype(BF16), wod_ref[...])
    merged = merged + _gate(h, w_ref, 3) * br

    y = _dot(merged.astype(BF16), wo_ref[...])
    xo_ref[...] = x + _rms(y, post_ref[...])


def _mixer_prompt(x, o, lse, weights, gws, gb_col):
    tm = MIX_TM
    steps = SEQ // tm
    row = lambda width: pl.BlockSpec((tm, width), lambda b, j: (b * steps + j, 0))
    tail = lambda n: pl.BlockSpec((1, n, B_WIDTH), lambda b, j: (b, 0, 0))
    return pl.pallas_call(
        functools.partial(_mixer_p_body, tm=tm),
        grid=(BATCH, steps),
        in_specs=([row(D_MODEL)] + [row(A_OUT)] * 6
                  + [_resident(s) for s in _MIX_WEIGHT_SHAPES]
                  + [_resident((C_GROUPS, C_CHUNK, C_CHUNK)), _resident((C_CHUNK, C_GROUPS))]),
        out_specs=[row(D_MODEL), tail(ZHALO), tail(GHALO)],
        out_shape=[jax.ShapeDtypeStruct((BATCH * SEQ, D_MODEL), F32),
                   jax.ShapeDtypeStruct((BATCH, ZHALO, B_WIDTH), F32),
                   jax.ShapeDtypeStruct((BATCH, GHALO, D_WIDTH), F32)],
        scratch_shapes=[pltpu.VMEM((tm + ZHALO, B_WIDTH), F32),
                        pltpu.VMEM((tm + GHALO, D_WIDTH), F32),
                        pltpu.VMEM((tm, D_WIDTH), F32)],
        compiler_params=_cparams("parallel", "arbitrary"),
        name="mixer_prompt",
    )(x, *o, *lse, *weights, gws, gb_col)


def _mixer_s_body(x_ref, attn_ref, sb_ref, sd_ref,
                  pre_ref, post_ref, w_ref, woa_ref, wob_ref, woc_ref, wod_ref, wo_ref,
                  cbw_ref, glng_ref, glnb_ref, cdw_ref, cdb_ref, dlng_ref, dlnb_ref,
                  wtab_ref, btab_ref,
                  xo_ref, nb_ref, vn_ref, glu_ref, *, sn):
    rows = DEC_SEQ * sn
    x = x_ref[...].reshape(rows, D_MODEL)
    h = _rms(x, pre_ref[...]).astype(BF16)
    slab = lambda a, t: a[t * sn:(t + 1) * sn, :]

    merged = _gate(h, w_ref, 0) * _dot(attn_ref[...].reshape(rows, A_OUT).astype(BF16), woa_ref[...])

    zc = _proj(h, w_ref, OFF_CGATE, B_WIDTH) * _proj(h, w_ref, OFF_BIN, B_WIDTH)
    zpad = [sb_ref[i] for i in range(B_CONV - 1)] + [slab(zc, t) for t in range(DEC_SEQ)]
    conv = jnp.concatenate(
        [sum(cbw_ref[k:k + 1, :] * zpad[t + k] for k in range(B_CONV)) for t in range(DEC_SEQ)], axis=0)
    br = _dot((_proj(h, w_ref, OFF_BGATE, B_WIDTH) * conv).astype(BF16), wob_ref[...])
    merged = merged + _gate(h, w_ref, 1) * br
    for i in range(B_CONV - 1):
        nb_ref[i] = zpad[DEC_SEQ + i]

    vn = _ln(_proj(h, w_ref, OFF_GV, C_WIDTH), glng_ref[...], glnb_ref[...])
    vn_ref[...] = vn.reshape(DEC_SEQ, sn, C_WIDTH)
    mixed = []
    for t in range(DEC_SEQ):
        acc = jnp.broadcast_to(btab_ref[t:t + 1, :], (sn, C_WIDTH))
        for s in range(t + 1):
            acc = acc + wtab_ref[t * DEC_SEQ + s:t * DEC_SEQ + s + 1, :] * slab(vn, s)
        mixed.append(acc)
    mixed = jnp.concatenate(mixed, axis=0)
    br = _dot((_proj(h, w_ref, OFF_U, C_WIDTH) * mixed).astype(BF16), woc_ref[...])
    merged = merged + _gate(h, w_ref, 2) * br

    glu = _proj(h, w_ref, OFF_GLUA, D_WIDTH) * jax.nn.sigmoid(_proj(h, w_ref, OFF_GLUG, D_WIDTH))
    glu_ref[...] = glu.reshape(DEC_SEQ, sn, D_WIDTH)
    gpad = [sd_ref[i] for i in range(D_CONV - 1)] + [slab(glu, t) for t in range(DEC_SEQ)]
    dc = []
    for t in range(DEC_SEQ):
        acc = jnp.broadcast_to(cdb_ref[...], (sn, D_WIDTH))
        for k in range(D_CONV):
            acc = acc + cdw_ref[k:k + 1, :] * gpad[t + k]
        dc.append(acc)
    dn = _ln(jnp.concatenate(dc, axis=0), dlng_ref[...], dlnb_ref[...])
    br = _dot(jax.nn.silu(dn).astype(BF16), wod_ref[...])
    merged = merged + _gate(h, w_ref, 3) * br

    y = _dot(merged.astype(BF16), wo_ref[...])
    xo_ref[...] = (x + _rms(y, post_ref[...])).reshape(DEC_SEQ, sn, D_MODEL)


def _mixer_sample(x, attn, state_b, state_d, weights, wtab, btab):
    sn = SAMP_SN
    tslab = lambda n, width: pl.BlockSpec((n, sn, width), lambda i: (0, i, 0))
    return pl.pallas_call(
        functools.partial(_mixer_s_body, sn=sn),
        grid=(DEC_BATCH // sn,),
        in_specs=([tslab(DEC_SEQ, D_MODEL), tslab(DEC_SEQ, A_OUT),
                   tslab(B_CONV - 1, B_WIDTH), tslab(D_CONV - 1, D_WIDTH)]
                  + [_resident(s) for s in _MIX_WEIGHT_SHAPES]
                  + [_resident((DEC_SEQ * DEC_SEQ, C_WIDTH)), _resident((DEC_SEQ, C_WIDTH))]),
        out_specs=[tslab(DEC_SEQ, D_MODEL), tslab(B_CONV - 1, B_WIDTH),
                   tslab(DEC_SEQ, C_WIDTH), tslab(DEC_SEQ, D_WIDTH)],
        out_shape=[jax.ShapeDtypeStruct((DEC_SEQ, DEC_BATCH, D_MODEL), F32),
                   jax.ShapeDtypeStruct((B_CONV - 1, DEC_BATCH, B_WIDTH), F32),
                   jax.ShapeDtypeStruct((DEC_SEQ, DEC_BATCH, C_WIDTH), F32),
                   jax.ShapeDtypeStruct((DEC_SEQ, DEC_BATCH, D_WIDTH), F32)],
        compiler_params=_cparams("parallel"),
        name="mixer_sample",
    )(x, attn, state_b, state_d, *weights, wtab, btab)


def kernel(x_prompt, x_sample, cache_attn_w128, cache_attn_w512, cache_attn_w2048, state_conv_b, state_conv_d, ffn1_pre_g, ffn1_post_g, ffn1_w_gate, ffn1_w_up, ffn1_w_down, mix_pre_g, mix_post_g, w_in, w_out_a, conv_b_w, w_out_b, gmlp_ln_g, gmlp_ln_b, gmlp_ws, gmlp_b, w_out_c, conv_d_w, conv_d_b, conv_d_ln_g, conv_d_ln_b, w_out_d, w_o, ffn2_pre_g, ffn2_post_g, ffn2_w_gate, ffn2_w_up, ffn2_w_down):
    bf = lambda a: a.astype(BF16)
    row = lambda a: a.reshape(1, -1)

    xp = x_prompt.reshape(BATCH * SEQ, D_MODEL)
    xs = x_sample.transpose(1, 0, 2).reshape(DEC_SEQ * DEC_BATCH, D_MODEL)
    kv_row = 2 * A_OUT
    c1 = cache_attn_w128.reshape(DEPTH, DEC_BATCH, 128, kv_row)
    c2 = cache_attn_w512.reshape(DEPTH, DEC_BATCH, 512, kv_row)
    c3 = cache_attn_w2048.reshape(DEPTH, DEC_BATCH, NK, 16 * kv_row)
    state_b_tm = state_conv_b.transpose(0, 2, 1, 3)
    state_d_tm = state_conv_d.transpose(0, 2, 1, 3)

    pa, sa = [[] for _ in A_GROUPS], [[] for _ in A_GROUPS]
    pb, sb, sc, pd, sd = [], [], [], [], []
    for l in range(DEPTH):
        ffn1 = (row(ffn1_pre_g[l]), row(ffn1_post_g[l]), bf(ffn1_w_gate[l]), bf(ffn1_w_up[l]), bf(ffn1_w_down[l]))
        ffn2 = (row(ffn2_pre_g[l]), row(ffn2_post_g[l]), bf(ffn2_w_gate[l]), bf(ffn2_w_up[l]), bf(ffn2_w_down[l]))
        w_in_l = bf(w_in[l])
        w_qkv, w_rest = w_in_l[:, :QKV_COLS], w_in_l[:, QKV_COLS:]
        pre = row(mix_pre_g[l])
        mix_w = (pre, row(mix_post_g[l]), w_rest,
                 bf(w_out_a[l]), bf(w_out_b[l]), bf(w_out_c[l]), bf(w_out_d[l]), bf(w_o[l]),
                 conv_b_w[l], row(gmlp_ln_g[l]), row(gmlp_ln_b[l]),
                 conv_d_w[l], row(conv_d_b[l]), row(conv_d_ln_g[l]), row(conv_d_ln_b[l]))

        xp = _ffn(xp, *ffn1)
        q, k, v = _qkv(xp, pre, w_qkv)
        outs = [_attn_prompt(q, k, v, g) for g in range(len(A_GROUPS))]
        xp, nb, nd = _mixer_prompt(xp, [o for o, _ in outs], [s for _, s in outs], mix_w,
                                   gmlp_ws[l], gmlp_b[l].T)
        xp = _ffn(xp, *ffn2)
        k4 = k.astype(F32).reshape(BATCH, SEQ, A_HEADS, HEAD_DIM)
        v4 = v.astype(F32).reshape(BATCH, SEQ, A_HEADS, HEAD_DIM)
        for g, (win, _) in enumerate(A_GROUPS):
            keep = min(win, SEQ)
            hs = slice(g * A_HPG, (g + 1) * A_HPG)
            pa[g].append(jnp.stack([k4[:, SEQ - keep:, hs], v4[:, SEQ - keep:, hs]], axis=2))
        pb.append(nb[:, ZHALO - (B_CONV - 1):])
        pd.append(nd[:, GHALO - (D_CONV - 1):])

        xs = _ffn(xs, *ffn1)
        qs, ks, vs = _qkv(xs, pre, w_qkv)
        seq_major = lambda a: a.astype(F32).reshape(DEC_SEQ, DEC_BATCH, A_WIDTH).transpose(1, 0, 2)
        qs, ks, vs = seq_major(qs), seq_major(ks), seq_major(vs)
        attn_s = _attn_sample(qs, ks, vs, c1, c2, c3, l).transpose(1, 0, 2)
        ws8 = gmlp_ws[l][:, :DEC_SEQ, :DEC_SEQ]
        wtab = jnp.repeat(ws8.transpose(1, 2, 0).reshape(DEC_SEQ * DEC_SEQ, C_GROUPS), C_GDIM, axis=1)
        btab = jnp.repeat(gmlp_b[l][:, :DEC_SEQ].T, C_GDIM, axis=1)
        xs3, nb_s, vn_s, glu_s = _mixer_sample(
            xs.reshape(DEC_SEQ, DEC_BATCH, D_MODEL), attn_s, state_b_tm[l], state_d_tm[l], mix_w, wtab, btab)
        xs = _ffn(xs3.reshape(DEC_SEQ * DEC_BATCH, D_MODEL), *ffn2)
        k4 = ks.astype(F32).reshape(DEC_BATCH, DEC_SEQ, A_HEADS, HEAD_DIM)
        v4 = vs.astype(F32).reshape(DEC_BATCH, DEC_SEQ, A_HEADS, HEAD_DIM)
        for g in range(len(A_GROUPS)):
            hs = slice(g * A_HPG, (g + 1) * A_HPG)
            sa[g].append(jnp.stack([k4[:, :, hs], v4[:, :, hs]], axis=2))
        sb.append(nb_s.transpose(1, 0, 2))
        sc.append(vn_s.transpose(1, 0, 2))
        sd.append(jnp.concatenate([state_conv_d[l][:, DEC_SEQ:], glu_s.transpose(1, 0, 2)], axis=1))

    st = lambda xs_: jnp.stack(xs_, axis=0)
    return (xp.reshape(BATCH, SEQ, D_MODEL),
            xs.reshape(DEC_SEQ, DEC_BATCH, D_MODEL).transpose(1, 0, 2),
            st(pa[0]), st(pa[1]), st(pa[2]), st(sa[0]), st(sa[1]), st(sa[2]),
            st(pb), st(sb), st(sc), st(pd), st(sd))
```

```python
import functools

import numpy as np
import jax
import jax.numpy as jnp
from jax import lax
from jax.experimental import pallas as pl
from jax.experimental.pallas import tpu as pltpu

D_MODEL = 1024
BATCH = 4
SEQ = 4096
DEPTH = 2
DEC_BATCH = 128
DEC_SEQ = 8
HEAD_DIM = 64
A_GROUPS = ((128, 1), (512, 4), (2048, 16))
A_HPG = 4
A_HEADS = A_HPG * len(A_GROUPS)
A_WIDTH = A_HEADS * HEAD_DIM
A_OUT = A_HPG * HEAD_DIM
NK = 128
B_WIDTH = 512
B_CONV = 3
C_WIDTH = 512
C_CHUNK = 128
C_GROUPS = 4
C_GDIM = 128
D_WIDTH = 512
D_CONV = 31
D_FF = 2816
EPS = 1e-6
QKV_COLS = 3 * A_WIDTH
OFF_BGATE, OFF_CGATE, OFF_BIN, OFF_U, OFF_GV, OFF_GLUA, OFF_GLUG, OFF_GATES = (
    0, 512, 1024, 1536, 2048, 2560, 3072, 3584)
REST_COLS = OFF_GATES + 4 * D_MODEL

F32 = jnp.float32
BF16 = jnp.bfloat16
NEG = -1e30
SUBLANES = 8
LANES = 128
V7X_VMEM_LIMIT_BYTES = 56 * 1024 * 1024

FFN_TM = 512
FFN_FC = 256
QKV_TM = 512
MIX_TM = 512
MIX_RC = 64
ZHALO = 8
GHALO = 32
ATT_TQ = 512
SAMP_SB = 2
SAMP_SN = 32


def _cparams(*sem):
    return pltpu.CompilerParams(dimension_semantics=sem, vmem_limit_bytes=V7X_VMEM_LIMIT_BYTES)


def _resident(shape):
    nd = len(shape)
    return pl.BlockSpec(shape, lambda *_: (0,) * nd, pipeline_mode=pl.Buffered(1))


def _dot(a, b):
    return jnp.dot(a, b, preferred_element_type=F32)


def _dot_nt(a, b):
    return lax.dot_general(a, b, (((1,), (1,)), ((), ())), preferred_element_type=F32)


def _rms(x, g):
    return x * lax.rsqrt(jnp.mean(x * x, axis=-1, keepdims=True) + EPS) * g


def _ln(x, g, b):
    mu = jnp.mean(x, axis=-1, keepdims=True)
    xc = x - mu
    var = jnp.mean(xc * xc, axis=-1, keepdims=True)
    return xc * lax.rsqrt(var + EPS) * g + b


def _ffn_body(x_ref, pre_ref, post_ref, wg_ref, wu_ref, wd_ref, o_ref):
    x = x_ref[...]
    h = _rms(x, pre_ref[...]).astype(BF16)
    acc = jnp.zeros(x.shape, F32)
    for c in range(D_FF // FFN_FC):
        sl = slice(c * FFN_FC, (c + 1) * FFN_FC)
        g = _dot(h, wg_ref[:, sl])
        u = _dot(h, wu_ref[:, sl])
        acc = acc + _dot((jax.nn.silu(g) * u).astype(BF16), wd_ref[sl, :])
    o_ref[...] = x + 0.5 * _rms(acc, post_ref[...])


def _ffn(x, pre_g, post_g, wg, wu, wd):
    rows = x.shape[0]
    tm = min(FFN_TM, rows)
    return pl.pallas_call(
        _ffn_body,
        grid=(rows // tm,),
        in_specs=[pl.BlockSpec((tm, D_MODEL), lambda i: (i, 0)),
                  _resident((1, D_MODEL)), _resident((1, D_MODEL)),
                  _resident((D_MODEL, D_FF)), _resident((D_MODEL, D_FF)), _resident((D_FF, D_MODEL))],
        out_specs=pl.BlockSpec((tm, D_MODEL), lambda i: (i, 0)),
        out_shape=jax.ShapeDtypeStruct((rows, D_MODEL), F32),
        compiler_params=_cparams("parallel"),
        name="ffn",
    )(x, pre_g, post_g, wg, wu, wd)


Q_SCALE = HEAD_DIM ** -0.5


def _qkv_s_body(x_ref, g_ref, w_ref, q_ref, k_ref, v_ref):
    h = _rms(x_ref[...], g_ref[...]).astype(BF16)
    q_ref[...] = (_dot(h, w_ref[:, 0:A_WIDTH]) * Q_SCALE).astype(BF16)
    k_ref[...] = _dot(h, w_ref[:, A_WIDTH:2 * A_WIDTH]).astype(BF16)
    v_ref[...] = _dot(h, w_ref[:, 2 * A_WIDTH:3 * A_WIDTH]).astype(BF16)


def _qkv_sample(x, g, w_qkv):
    rows = x.shape[0]
    tm = min(QKV_TM, rows)
    out = jax.ShapeDtypeStruct((rows, A_WIDTH), BF16)
    spec = pl.BlockSpec((tm, A_WIDTH), lambda i: (i, 0))
    return pl.pallas_call(
        _qkv_s_body,
        grid=(rows // tm,),
        in_specs=[pl.BlockSpec((tm, D_MODEL), lambda i: (i, 0)),
                  _resident((1, D_MODEL)), _resident((D_MODEL, QKV_COLS))],
        out_specs=[spec, spec, spec],
        out_shape=[out, out, out],
        compiler_params=_cparams("parallel"),
        name="qkv_sample",
    )(x, g, w_qkv)


def _qkv_p_body(x_ref, g_ref, w_ref, o1_ref, o2_ref, o3_ref, acc_ref, *, tm):
    h = _rms(x_ref[...], g_ref[...]).astype(BF16)
    res = _dot(h, w_ref[...])
    for c in range(QKV_COLS // LANES):
        acc_ref[c] = res[:, c * LANES:(c + 1) * LANES]
    for g, (_, dil) in enumerate(A_GROUPS):
        out = (o1_ref, o2_ref, o3_ref)[g]
        for c in range(3 * A_OUT // LANES):
            part, sub = divmod(c * LANES, A_OUT)
            src = (part * A_WIDTH + g * A_OUT + sub) // LANES
            for r in range(dil):
                val = acc_ref[src] if dil == 1 else acc_ref[src, pl.ds(r, tm // dil, stride=dil), :]
                if part == 0:
                    val = val * Q_SCALE
                out[0, r, :, c * LANES:(c + 1) * LANES] = val.astype(BF16)


def _qkv_prompt(x, g, w_qkv):
    tm = QKV_TM
    steps = SEQ // tm
    dils = [dil for _, dil in A_GROUPS]
    return pl.pallas_call(
        functools.partial(_qkv_p_body, tm=tm),
        grid=(BATCH, steps),
        in_specs=[pl.BlockSpec((tm, D_MODEL), lambda b, j: (b * steps + j, 0)),
                  _resident((1, D_MODEL)), _resident((D_MODEL, QKV_COLS))],
        out_specs=[pl.BlockSpec((1, dil, tm // dil, 3 * A_OUT), lambda b, j: (b, 0, j, 0)) for dil in dils],
        out_shape=[jax.ShapeDtypeStruct((BATCH, dil, SEQ // dil, 3 * A_OUT), BF16) for dil in dils],
        scratch_shapes=[pltpu.VMEM((QKV_COLS // LANES, tm, LANES), F32)],
        compiler_params=_cparams("parallel", "parallel"),
        name="qkv_prompt",
    )(x, g, w_qkv)


def _alibi_slopes():
    return np.exp2(-8.0 * np.arange(1, A_HEADS + 1, dtype=np.float64) / A_HEADS)


def _prompt_bias(group):
    _, dil = A_GROUPS[group]
    rel = NK + np.arange(NK)[:, None] - np.arange(2 * NK)[None, :]
    valid = (rel >= 0) & (rel <= NK)
    slopes = _alibi_slopes()[group * A_HPG:(group + 1) * A_HPG]
    bias = -slopes[:, None, None] * (dil * rel)[None].astype(np.float64)
    return np.where(valid[None], bias, NEG).astype(np.float32).reshape(A_HPG * NK, 2 * NK)


def _attn_p_body(q_ref, kp_ref, kc_ref, vp_ref, vc_ref, bias_ref, o_ref, lse_ref, *, tq):
    first = (pl.program_id(2) == 0).astype(F32)
    lane_head = lax.broadcasted_iota(jnp.int32, (1, A_OUT), 1) // HEAD_DIM
    key_col = lax.broadcasted_iota(jnp.int32, (1, 2 * NK), 1)
    no_prev = jnp.where(key_col < NK, NEG, 0.0) * first
    nblk = tq // NK
    blk = lambda ref, j: ref[0, 0, j * NK:(j + 1) * NK, :]
    scores, values = [], []
    for j in range(nblk):
        qj = blk(q_ref, j)
        qs = jnp.concatenate([jnp.where(lane_head == h, qj, jnp.zeros_like(qj)) for h in range(A_HPG)], axis=0)
        k_prev, v_prev = (kp_ref[0, 0], vp_ref[0, 0]) if j == 0 else (blk(kc_ref, j - 1), blk(vc_ref, j - 1))
        s = _dot_nt(qs, jnp.concatenate([k_prev, blk(kc_ref, j)], axis=0)) + bias_ref[...]
        scores.append(s + no_prev if j == 0 else s)
        values.append(jnp.concatenate([v_prev, blk(vc_ref, j)], axis=0))
    stats = []
    for j in range(nblk):
        m = jnp.max(scores[j], axis=-1, keepdims=True)
        p = jnp.exp(scores[j] - m)
        den = jnp.sum(p, axis=-1, keepdims=True)
        scores[j] = p.astype(BF16)
        stats.append((den, m + jnp.log(den)))
    for j in range(nblk):
        den, lse = stats[j]
        pv = _dot(scores[j], values[j]) / den
        o_acc = jnp.zeros((NK, A_OUT), F32)
        l_acc = jnp.zeros((NK, A_OUT), F32)
        for h in range(A_HPG):
            rows = slice(h * NK, (h + 1) * NK)
            o_acc = jnp.where(lane_head == h, pv[rows], o_acc)
            l_acc = jnp.where(lane_head == h, lse[rows], l_acc)
        o_ref[0, 0, j * NK:(j + 1) * NK, :] = o_acc
        lse_ref[0, 0, j * NK:(j + 1) * NK, :] = l_acc


def _attn_prompt(qkv, group):
    _, dil = A_GROUPS[group]
    sub = SEQ // dil
    tq = min(ATT_TQ, sub)
    cur = lambda col: pl.BlockSpec((1, 1, tq, A_OUT), lambda b, r, i: (b, r, i, col))
    prev = lambda col: pl.BlockSpec((1, 1, NK, A_OUT),
                                    lambda b, r, i: (b, r, jnp.maximum(i * (tq // NK) - 1, 0), col))
    out = jax.ShapeDtypeStruct((BATCH, dil, sub, A_OUT), F32)
    return pl.pallas_call(
        functools.partial(_attn_p_body, tq=tq),
        grid=(BATCH, dil, sub // tq),
        in_specs=[cur(0), prev(1), cur(1), prev(2), cur(2), _resident((A_HPG * NK, 2 * NK))],
        out_specs=[cur(0), cur(0)],
        out_shape=[out, out],
        compiler_params=_cparams("parallel", "parallel", "arbitrary"),
        name=f"attn_prompt_g{group}",
    )(qkv, qkv, qkv, qkv, qkv, jnp.asarray(_prompt_bias(group)))


SAMP_CACHE_LEN = tuple(win for win, _ in A_GROUPS)
SAMP_NEW_SLOT = 128
SAMP_SEG = []
_off = 0
for _len in SAMP_CACHE_LEN:
    SAMP_SEG.append((_off, _off + _len, _off + _len + SAMP_NEW_SLOT))
    _off += _len + SAMP_NEW_SLOT
SAMP_KEYS = _off
SAMP_Q = A_HPG * DEC_SEQ
NEW_PAD = 16


def _sample_bias():
    slopes = _alibi_slopes()
    bias = np.full((SAMP_Q, SAMP_KEYS), NEG, np.float64)
    for c in range(SAMP_Q):
        h, t = divmod(c, DEC_SEQ)
        for g, (win, dil) in enumerate(A_GROUPS):
            c0, n0, _ = SAMP_SEG[g]
            sl = slopes[g * A_HPG + h]
            dist = win + t - np.arange(win)
            ok = (dist % dil == 0) & (dist <= win)
            bias[c, c0:c0 + win] = np.where(ok, -sl * dist, NEG)
            for t2 in range(t + 1):
                if (t - t2) % dil == 0:
                    bias[c, n0 + t2] = -sl * (t - t2)
    return bias.astype(np.float32)


def _attn_s_body(q_ref, k_ref, v_ref, c1_ref, c2_ref, c3_ref, bias_ref, o_ref, knew, vnew, *, sb):
    @pl.when(pl.program_id(0) == 0)
    def _():
        knew[...] = jnp.zeros(knew.shape, BF16)
        vnew[...] = jnp.zeros(vnew.shape, BF16)

    lane_head = lax.broadcasted_iota(jnp.int32, (1, A_OUT), 1) // HEAD_DIM
    pad = jnp.zeros((NEW_PAD - DEC_SEQ, A_OUT), F32)
    caches = (c1_ref, c2_ref, c3_ref)

    def one_seq(n, carry):
        qn, kn, vn = q_ref[n], k_ref[n], v_ref[n]
        scores = []
        for g in range(3):
            c0, n0, e0 = SAMP_SEG[g]
            cols = slice(g * A_OUT, (g + 1) * A_OUT)
            knew[g, 0:NEW_PAD, :] = jnp.concatenate([kn[:, cols], pad], axis=0).astype(BF16)
            vnew[g, 0:NEW_PAD, :] = jnp.concatenate([vn[:, cols], pad], axis=0).astype(BF16)
            qbd = jnp.concatenate([jnp.where(lane_head == h, qn[:, cols], 0.0) for h in range(A_HPG)],
                                  axis=0).astype(BF16)
            k_t = caches[g][0, n, 0:A_OUT, :].astype(BF16)
            scores.append(_dot(qbd, k_t) + bias_ref[:, c0:n0])
            scores.append(_dot_nt(qbd, knew[g]) + bias_ref[:, n0:e0])
        m = functools.reduce(jnp.maximum, [jnp.max(s, axis=-1, keepdims=True) for s in scores])
        den = jnp.zeros((SAMP_Q, 1), F32)
        out = jnp.zeros((SAMP_Q, A_OUT), F32)
        for g in range(3):
            p_c = jnp.exp(scores[2 * g] - m)
            p_n = jnp.exp(scores[2 * g + 1] - m)
            den = den + jnp.sum(p_c, axis=-1, keepdims=True) + jnp.sum(p_n, axis=-1, keepdims=True)
            v_t = caches[g][0, n, A_OUT:2 * A_OUT, :].astype(BF16)
            out = out + _dot_nt(p_c.astype(BF16), v_t) + _dot(p_n.astype(BF16), vnew[g])
        out = out / den
        attn = jnp.zeros((DEC_SEQ, A_OUT), F32)
        for h in range(A_HPG):
            attn = jnp.where(lane_head == h, out[h * DEC_SEQ:(h + 1) * DEC_SEQ, :], attn)
        o_ref[n] = attn
        return carry

    lax.fori_loop(0, sb, one_seq, 0)


def _attn_sample(q, k, v, c1, c2, c3, layer):
    sb = SAMP_SB
    qspec = pl.BlockSpec((sb, DEC_SEQ, A_WIDTH), lambda i: (i, 0, 0))
    cspec = lambda n: pl.BlockSpec((1, sb, 2 * A_OUT, n), lambda i: (layer, i, 0, 0))
    return pl.pallas_call(
        functools.partial(_attn_s_body, sb=sb),
        grid=(DEC_BATCH // sb,),
        in_specs=[qspec, qspec, qspec] + [cspec(n) for n in SAMP_CACHE_LEN] + [_resident((SAMP_Q, SAMP_KEYS))],
        out_specs=pl.BlockSpec((sb, DEC_SEQ, A_OUT), lambda i: (i, 0, 0)),
        out_shape=jax.ShapeDtypeStruct((DEC_BATCH, DEC_SEQ, A_OUT), F32),
        scratch_shapes=[pltpu.VMEM((3, SAMP_NEW_SLOT, A_OUT), BF16), pltpu.VMEM((3, SAMP_NEW_SLOT, A_OUT), BF16)],
        compiler_params=_cparams("arbitrary"),
        name="attn_sample",
    )(q, k, v, c1, c2, c3, jnp.asarray(_sample_bias()))


def _proj(h, w_ref, off, width):
    return _dot(h, w_ref[:, off:off + width])


def _gate(h, w_ref, branch):
    return jax.nn.sigmoid(_proj(h, w_ref, OFF_GATES + branch * D_MODEL, D_MODEL))


_MIX_WEIGHT_SHAPES = (
    (1, D_MODEL), (1, D_MODEL),
    (D_MODEL, REST_COLS),
    (A_OUT, D_MODEL), (B_WIDTH, D_MODEL), (C_WIDTH, D_MODEL), (D_WIDTH, D_MODEL), (D_MODEL, D_MODEL),
    (B_CONV, B_WIDTH), (1, C_WIDTH), (1, C_WIDTH),
    (D_CONV, D_WIDTH), (1, D_WIDTH), (1, D_WIDTH), (1, D_WIDTH),
)


def _mixer_p_body(x_ref, o1_ref, o2_ref, o3_ref, l1_ref, l2_ref, l3_ref,
                  pre_ref, post_ref, w_ref, woa_ref, wob_ref, woc_ref, wod_ref, wo_ref,
                  cbw_ref, glng_ref, glnb_ref, cdw_ref, cdb_ref, dlng_ref, dlnb_ref,
                  gws_ref, gbc_ref,
                  xo_ref, nb_ref, nd_ref, zbuf, gbuf, nat, *, tm):
    @pl.when(pl.program_id(1) == 0)
    def _():
        zbuf[0:ZHALO, :] = jnp.zeros((ZHALO, B_WIDTH), F32)
        gbuf[0:GHALO, :] = jnp.zeros((GHALO, D_WIDTH), F32)

    x = x_ref[...]
    h = _rms(x, pre_ref[...]).astype(BF16)

    def natural(ref, slot, dil):
        halves = A_OUT // LANES
        for r in range(dil):
            for c in range(halves):
                nat[slot * halves + c, pl.ds(r, tm // dil, stride=dil), :] = ref[0, r, :, c * LANES:(c + 1) * LANES]
        return jnp.concatenate([nat[slot * halves + c] for c in range(halves)], axis=1)

    dil2, dil3 = A_GROUPS[1][1], A_GROUPS[2][1]
    o1, l1 = o1_ref[...], l1_ref[...]
    o2, l2 = natural(o2_ref, 0, dil2), natural(l2_ref, 1, dil2)
    o3, l3 = natural(o3_ref, 2, dil3), natural(l3_ref, 3, dil3)
    mx = jnp.maximum(jnp.maximum(l1, l2), l3)
    e1, e2, e3 = jnp.exp(l1 - mx), jnp.exp(l2 - mx), jnp.exp(l3 - mx)
    attn = (e1 * o1 + e2 * o2 + e3 * o3) / (e1 + e2 + e3)
    merged = _gate(h, w_ref, 0) * _dot(attn.astype(BF16), woa_ref[...])

    zbuf[ZHALO:ZHALO + tm, :] = _proj(h, w_ref, OFF_CGATE, B_WIDTH) * _proj(h, w_ref, OFF_BIN, B_WIDTH)
    conv = jnp.zeros((tm, B_WIDTH), F32)
    for k in range(B_CONV):
        start = ZHALO - (B_CONV - 1) + k
        conv = conv + cbw_ref[k:k + 1, :] * zbuf[start:start + tm, :]
    br = _dot((_proj(h, w_ref, OFF_BGATE, B_WIDTH) * conv).astype(BF16), wob_ref[...])
    merged = merged + _gate(h, w_ref, 1) * br
    tail = zbuf[tm:tm + ZHALO, :]
    nb_ref[0] = tail
    zbuf[0:ZHALO, :] = tail

    vn = _ln(_proj(h, w_ref, OFF_GV, C_WIDTH), glng_ref[...], glnb_ref[...]).astype(BF16)
    r_i = lax.broadcasted_iota(jnp.int32, (C_CHUNK, C_CHUNK), 0)
    c_i = lax.broadcasted_iota(jnp.int32, (C_CHUNK, C_CHUNK), 1)
    ws = [jnp.where(r_i >= c_i, gws_ref[g], 0.0).astype(BF16) for g in range(C_GROUPS)]
    chunks = []
    for c in range(tm // C_CHUNK):
        rows = slice(c * C_CHUNK, (c + 1) * C_CHUNK)
        chunks.append(jnp.concatenate(
            [_dot(ws[g], vn[rows, g * C_GDIM:(g + 1) * C_GDIM]) + gbc_ref[:, g:g + 1]
             for g in range(C_GROUPS)], axis=1))
    mixed = jnp.concatenate(chunks, axis=0)
    br = _dot((_proj(h, w_ref, OFF_U, C_WIDTH) * mixed).astype(BF16), woc_ref[...])
    merged = merged + _gate(h, w_ref, 2) * br

    gbuf[GHALO:GHALO + tm, :] = (_proj(h, w_ref, OFF_GLUA, D_WIDTH)
                                 * jax.nn.sigmoid(_proj(h, w_ref, OFF_GLUG, D_WIDTH)))

    wrows = MIX_RC + GHALO
    dc = []
    for ci in range(tm // MIX_RC):
        win = gbuf[ci * MIX_RC:ci * MIX_RC + wrows, :]
        phase = [win] + [pltpu.roll(win, wrows - s, axis=0) for s in range(1, SUBLANES)]
        acc = jnp.broadcast_to(cdb_ref[...], (MIX_RC, D_WIDTH))
        for k in range(D_CONV):
            a, s = divmod(GHALO - (D_CONV - 1) + k, SUBLANES)
            acc = acc + cdw_ref[k:k + 1, :] * phase[s][a * SUBLANES:a * SUBLANES + MIX_RC, :]
        dc.append(acc)
    tail = gbuf[tm:tm + GHALO, :]
    nd_ref[0] = tail
    gbuf[0:GHALO, :] = tail
    dn = _ln(jnp.concatenate(dc, axis=0), dlng_ref[...], dlnb_ref[...])
    br = _dot(jax.nn.silu(dn).astype(BF16), wod_ref[...])
    merged = merged + _gate(h, w_ref, 3) * br

    y = _dot(merged.astype(BF16), wo_ref[...])
    xo_ref[...] = x + _rms(y, post_ref[...])


def _mixer_prompt(x, o, lse, weights, gws, gb_col):
    tm = MIX_TM
    steps = SEQ // tm
    row = lambda width: pl.BlockSpec((tm, width), lambda b, j: (b * steps + j, 0))
    tail = lambda n: pl.BlockSpec((1, n, B_WIDTH), lambda b, j: (b, 0, 0))
    res = lambda dil: pl.BlockSpec((1, dil, tm // dil, A_OUT), lambda b, j: (b, 0, j, 0))
    attn_specs = [row(A_OUT)] + [res(dil) for _, dil in A_GROUPS[1:]]
    flat = lambda a: a.reshape(BATCH * SEQ, A_OUT)
    return pl.pallas_call(
        functools.partial(_mixer_p_body, tm=tm),
        grid=(BATCH, steps),
        in_specs=([row(D_MODEL)] + attn_specs + attn_specs
                  + [_resident(s) for s in _MIX_WEIGHT_SHAPES]
                  + [_resident((C_GROUPS, C_CHUNK, C_CHUNK)), _resident((C_CHUNK, C_GROUPS))]),
        out_specs=[row(D_MODEL), tail(ZHALO), tail(GHALO)],
        out_shape=[jax.ShapeDtypeStruct((BATCH * SEQ, D_MODEL), F32),
                   jax.ShapeDtypeStruct((BATCH, ZHALO, B_WIDTH), F32),
                   jax.ShapeDtypeStruct((BATCH, GHALO, D_WIDTH), F32)],
        scratch_shapes=[pltpu.VMEM((tm + ZHALO, B_WIDTH), F32),
                        pltpu.VMEM((tm + GHALO, D_WIDTH), F32),
                        pltpu.VMEM((4 * A_OUT // LANES, tm, LANES), F32)],
        compiler_params=_cparams("parallel", "arbitrary"),
        name="mixer_prompt",
    )(x, flat(o[0]), o[1], o[2], flat(lse[0]), lse[1], lse[2], *weights, gws, gb_col)


def _mixer_s_body(x_ref, attn_ref, sb_ref, sd_ref,
                  pre_ref, post_ref, w_ref, woa_ref, wob_ref, woc_ref, wod_ref, wo_ref,
                  cbw_ref, glng_ref, glnb_ref, cdw_ref, cdb_ref, dlng_ref, dlnb_ref,
                  wtab_ref, btab_ref,
                  xo_ref, nb_ref, vn_ref, glu_ref, *, sn):
    rows = DEC_SEQ * sn
    x = x_ref[...].reshape(rows, D_MODEL)
    h = _rms(x, pre_ref[...]).astype(BF16)
    slab = lambda a, t: a[t * sn:(t + 1) * sn, :]

    merged = _gate(h, w_ref, 0) * _dot(attn_ref[...].reshape(rows, A_OUT).astype(BF16), woa_ref[...])

    zc = _proj(h, w_ref, OFF_CGATE, B_WIDTH) * _proj(h, w_ref, OFF_BIN, B_WIDTH)
    zpad = [sb_ref[i] for i in range(B_CONV - 1)] + [slab(zc, t) for t in range(DEC_SEQ)]
    conv = jnp.concatenate(
        [sum(cbw_ref[k:k + 1, :] * zpad[t + k] for k in range(B_CONV)) for t in range(DEC_SEQ)], axis=0)
    br = _dot((_proj(h, w_ref, OFF_BGATE, B_WIDTH) * conv).astype(BF16), wob_ref[...])
    merged = merged + _gate(h, w_ref, 1) * br
    for i in range(B_CONV - 1):
        nb_ref[i] = zpad[DEC_SEQ + i]

    vn = _ln(_proj(h, w_ref, OFF_GV, C_WIDTH), glng_ref[...], glnb_ref[...])
    vn_ref[...] = vn.reshape(DEC_SEQ, sn, C_WIDTH)
    mixed = []
    for t in range(DEC_SEQ):
        acc = jnp.broadcast_to(btab_ref[t:t + 1, :], (sn, C_WIDTH))
        for s in range(t + 1):
            acc = acc + wtab_ref[t * DEC_SEQ + s:t * DEC_SEQ + s + 1, :] * slab(vn, s)
        mixed.append(acc)
    mixed = jnp.concatenate(mixed, axis=0)
    br = _dot((_proj(h, w_ref, OFF_U, C_WIDTH) * mixed).astype(BF16), woc_ref[...])
    merged = merged + _gate(h, w_ref, 2) * br

    glu = _proj(h, w_ref, OFF_GLUA, D_WIDTH) * jax.nn.sigmoid(_proj(h, w_ref, OFF_GLUG, D_WIDTH))
    glu_ref[...] = glu.reshape(DEC_SEQ, sn, D_WIDTH)
    gpad = [sd_ref[i] for i in range(D_CONV - 1)] + [slab(glu, t) for t in range(DEC_SEQ)]
    dc = []
    for t in range(DEC_SEQ):
        acc = jnp.broadcast_to(cdb_ref[...], (sn, D_WIDTH))
        for k in range(D_CONV):
            acc = acc + cdw_ref[k:k + 1, :] * gpad[t + k]
        dc.append(acc)
    dn = _ln(jnp.concatenate(dc, axis=0), dlng_ref[...], dlnb_ref[...])
    br = _dot(jax.nn.silu(dn).astype(BF16), wod_ref[...])
    merged = merged + _gate(h, w_ref, 3) * br

    y = _dot(merged.astype(BF16), wo_ref[...])
    xo_ref[...] = (x + _rms(y, post_ref[...])).reshape(DEC_SEQ, sn, D_MODEL)


def _mixer_sample(x, attn, state_b, state_d, weights, wtab, btab):
    sn = SAMP_SN
    tslab = lambda n, width: pl.BlockSpec((n, sn, width), lambda i: (0, i, 0))
    return pl.pallas_call(
        functools.partial(_mixer_s_body, sn=sn),
        grid=(DEC_BATCH // sn,),
        in_specs=([tslab(DEC_SEQ, D_MODEL), tslab(DEC_SEQ, A_OUT),
                   tslab(B_CONV - 1, B_WIDTH), tslab(D_CONV - 1, D_WIDTH)]
                  + [_resident(s) for s in _MIX_WEIGHT_SHAPES]
                  + [_resident((DEC_SEQ * DEC_SEQ, C_WIDTH)), _resident((DEC_SEQ, C_WIDTH))]),
        out_specs=[tslab(DEC_SEQ, D_MODEL), tslab(B_CONV - 1, B_WIDTH),
                   tslab(DEC_SEQ, C_WIDTH), tslab(DEC_SEQ, D_WIDTH)],
        out_shape=[jax.ShapeDtypeStruct((DEC_SEQ, DEC_BATCH, D_MODEL), F32),
                   jax.ShapeDtypeStruct((B_CONV - 1, DEC_BATCH, B_WIDTH), F32),
                   jax.ShapeDtypeStruct((DEC_SEQ, DEC_BATCH, C_WIDTH), F32),
                   jax.ShapeDtypeStruct((DEC_SEQ, DEC_BATCH, D_WIDTH), F32)],
        compiler_params=_cparams("parallel"),
        name="mixer_sample",
    )(x, attn, state_b, state_d, *weights, wtab, btab)


def kernel(x_prompt, x_sample, cache_attn_w128, cache_attn_w512, cache_attn_w2048, state_conv_b, state_conv_d, ffn1_pre_g, ffn1_post_g, ffn1_w_gate, ffn1_w_up, ffn1_w_down, mix_pre_g, mix_post_g, w_in, w_out_a, conv_b_w, w_out_b, gmlp_ln_g, gmlp_ln_b, gmlp_ws, gmlp_b, w_out_c, conv_d_w, conv_d_b, conv_d_ln_g, conv_d_ln_b, w_out_d, w_o, ffn2_pre_g, ffn2_post_g, ffn2_w_gate, ffn2_w_up, ffn2_w_down):
    bf = lambda a: a.astype(BF16)
    row = lambda a: a.reshape(1, -1)

    xp = x_prompt.reshape(BATCH * SEQ, D_MODEL)
    xs = x_sample.transpose(1, 0, 2).reshape(DEC_SEQ * DEC_BATCH, D_MODEL)
    time_minor = lambda c: c.transpose(0, 1, 3, 4, 5, 2).reshape(DEPTH, DEC_BATCH, 2 * A_OUT, c.shape[2])
    c1, c2, c3 = time_minor(cache_attn_w128), time_minor(cache_attn_w512), time_minor(cache_attn_w2048)
    state_b_tm = state_conv_b.transpose(0, 2, 1, 3)
    state_d_tm = state_conv_d.transpose(0, 2, 1, 3)

    pa, sa = [[] for _ in A_GROUPS], [[] for _ in A_GROUPS]
    pb, sb, sc, pd, sd = [], [], [], [], []
    for l in range(DEPTH):
        ffn1 = (row(ffn1_pre_g[l]), row(ffn1_post_g[l]), bf(ffn1_w_gate[l]), bf(ffn1_w_up[l]), bf(ffn1_w_down[l]))
        ffn2 = (row(ffn2_pre_g[l]), row(ffn2_post_g[l]), bf(ffn2_w_gate[l]), bf(ffn2_w_up[l]), bf(ffn2_w_down[l]))
        w_in_l = bf(w_in[l])
        w_qkv, w_rest = w_in_l[:, :QKV_COLS], w_in_l[:, QKV_COLS:]
        pre = row(mix_pre_g[l])
        mix_w = (pre, row(mix_post_g[l]), w_rest,
                 bf(w_out_a[l]), bf(w_out_b[l]), bf(w_out_c[l]), bf(w_out_d[l]), bf(w_o[l]),
                 conv_b_w[l], row(gmlp_ln_g[l]), row(gmlp_ln_b[l]),
                 conv_d_w[l], row(conv_d_b[l]), row(conv_d_ln_g[l]), row(conv_d_ln_b[l]))

        xp = _ffn(xp, *ffn1)
        qkv = _qkv_prompt(xp, pre, w_qkv)
        outs = [_attn_prompt(qkv[g], g) for g in range(len(A_GROUPS))]
        xp, nb, nd = _mixer_prompt(xp, [o for o, _ in outs], [s for _, s in outs], mix_w,
                                   gmlp_ws[l], gmlp_b[l].T)
        xp = _ffn(xp, *ffn2)
        for g, (win, dil) in enumerate(A_GROUPS):
            keep = min(win, SEQ)
            kv = qkv[g][:, :, (SEQ - keep) // dil:, A_OUT:].astype(F32)
            kv = kv.transpose(0, 2, 1, 3).reshape(BATCH, keep, 2, A_HPG, HEAD_DIM)
            pa[g].append(kv)
        pb.append(nb[:, ZHALO - (B_CONV - 1):])
        pd.append(nd[:, GHALO - (D_CONV - 1):])

        xs = _ffn(xs, *ffn1)
        qs, ks, vs = _qkv_sample(xs, pre, w_qkv)
        seq_major = lambda a: a.astype(F32).reshape(DEC_SEQ, DEC_BATCH, A_WIDTH).transpose(1, 0, 2)
        qs, ks, vs = seq_major(qs), seq_major(ks), seq_major(vs)
        attn_s = _attn_sample(qs, ks, vs, c1, c2, c3, l).transpose(1, 0, 2)
        ws8 = gmlp_ws[l][:, :DEC_SEQ, :DEC_SEQ]
        wtab = jnp.repeat(ws8.transpose(1, 2, 0).reshape(DEC_SEQ * DEC_SEQ, C_GROUPS), C_GDIM, axis=1)
        btab = jnp.repeat(gmlp_b[l][:, :DEC_SEQ].T, C_GDIM, axis=1)
        xs3, nb_s, vn_s, glu_s = _mixer_sample(
            xs.reshape(DEC_SEQ, DEC_BATCH, D_MODEL), attn_s, state_b_tm[l], state_d_tm[l], mix_w, wtab, btab)
        xs = _ffn(xs3.reshape(DEC_SEQ * DEC_BATCH, D_MODEL), *ffn2)
        k4 = ks.astype(F32).reshape(DEC_BATCH, DEC_SEQ, A_HEADS, HEAD_DIM)
        v4 = vs.astype(F32).reshape(DEC_BATCH, DEC_SEQ, A_HEADS, HEAD_DIM)
        for g in range(len(A_GROUPS)):
            hs = slice(g * A_HPG, (g + 1) * A_HPG)
            sa[g].append(jnp.stack([k4[:, :, hs], v4[:, :, hs]], axis=2))
        sb.append(nb_s.transpose(1, 0, 2))
        sc.append(vn_s.transpose(1, 0, 2))
        sd.append(jnp.concatenate([state_conv_d[l][:, DEC_SEQ:], glu_s.transpose(1, 0, 2)], axis=1))

    st = lambda xs_: jnp.stack(xs_, axis=0)
    return (xp.reshape(BATCH, SEQ, D_MODEL),
            xs.reshape(DEC_SEQ, DEC_BATCH, D_MODEL).transpose(1, 0, 2),
            st(pa[0]), st(pa[1]), st(pa[2]), st(sa[0]), st(sa[1]), st(sa[2]),
            st(pb), st(sb), st(sc), st(pd), st(sd))
```

```python
import functools

import numpy as np
import jax
import jax.numpy as jnp
from jax import lax
from jax.experimental import pallas as pl
from jax.experimental.pallas import tpu as pltpu

D_MODEL = 1024
BATCH = 4
SEQ = 4096
DEPTH = 2
DEC_BATCH = 128
DEC_SEQ = 8
HEAD_DIM = 64
A_GROUPS = ((128, 1), (512, 4), (2048, 16))
A_HPG = 4
A_HEADS = A_HPG * len(A_GROUPS)
A_WIDTH = A_HEADS * HEAD_DIM
A_OUT = A_HPG * HEAD_DIM
NK = 128
B_WIDTH = 512
B_CONV = 3
C_WIDTH = 512
C_CHUNK = 128
C_GROUPS = 4
C_GDIM = 128
D_WIDTH = 512
D_CONV = 31
D_FF = 2816
EPS = 1e-6
QKV_COLS = 3 * A_WIDTH
OFF_BGATE, OFF_CGATE, OFF_BIN, OFF_U, OFF_GV, OFF_GLUA, OFF_GLUG, OFF_GATES = (
    0, 512, 1024, 1536, 2048, 2560, 3072, 3584)
REST_COLS = OFF_GATES + 4 * D_MODEL

F32 = jnp.float32
BF16 = jnp.bfloat16
NEG = -1e30
SUBLANES = 8
LANES = 128
V7X_VMEM_LIMIT_BYTES = 56 * 1024 * 1024

FFN_TM = 512
FFN_FC = 256
QKV_TM = 512
MIX_TM = 512
MIX_RC = 64
ZHALO = 8
GHALO = 32
ATT_TQ = 512
SAMP_SB = 2
SAMP_SN = 32


def _cparams(*sem):
    return pltpu.CompilerParams(dimension_semantics=sem, vmem_limit_bytes=V7X_VMEM_LIMIT_BYTES)


def _resident(shape):
    nd = len(shape)
    return pl.BlockSpec(shape, lambda *_: (0,) * nd, pipeline_mode=pl.Buffered(1))


def _of_layer(shape, layer):
    nd = len(shape)
    return pl.BlockSpec((None,) + tuple(shape), lambda *_: (layer,) + (0,) * nd, pipeline_mode=pl.Buffered(1))


def _dot(a, b):
    return jnp.dot(a, b, preferred_element_type=F32)


def _dot_nt(a, b):
    return lax.dot_general(a, b, (((1,), (1,)), ((), ())), preferred_element_type=F32)


def _rms(x, g):
    return x * lax.rsqrt(jnp.mean(x * x, axis=-1, keepdims=True) + EPS) * g


def _ln(x, g, b):
    mu = jnp.mean(x, axis=-1, keepdims=True)
    xc = x - mu
    var = jnp.mean(xc * xc, axis=-1, keepdims=True)
    return xc * lax.rsqrt(var + EPS) * g + b


def _ffn_body(x_ref, pre_ref, post_ref, wg_ref, wu_ref, wd_ref, o_ref):
    x = x_ref[...]
    h = _rms(x, pre_ref[...]).astype(BF16)
    acc = jnp.zeros(x.shape, F32)
    for c in range(D_FF // FFN_FC):
        sl = slice(c * FFN_FC, (c + 1) * FFN_FC)
        g = _dot(h, wg_ref[:, sl])
        u = _dot(h, wu_ref[:, sl])
        acc = acc + _dot((jax.nn.silu(g) * u).astype(BF16), wd_ref[sl, :])
    o_ref[...] = x + 0.5 * _rms(acc, post_ref[...])


def _ffn(x, pre_g, post_g, wg, wu, wd, layer):
    rows = x.shape[0]
    tm = min(FFN_TM, rows)
    return pl.pallas_call(
        _ffn_body,
        grid=(rows // tm,),
        in_specs=[pl.BlockSpec((tm, D_MODEL), lambda i: (i, 0)),
                  _of_layer((1, D_MODEL), layer), _of_layer((1, D_MODEL), layer),
                  _of_layer((D_MODEL, D_FF), layer), _of_layer((D_MODEL, D_FF), layer),
                  _of_layer((D_FF, D_MODEL), layer)],
        out_specs=pl.BlockSpec((tm, D_MODEL), lambda i: (i, 0)),
        out_shape=jax.ShapeDtypeStruct((rows, D_MODEL), F32),
        compiler_params=_cparams("parallel"),
        name="ffn",
    )(x, pre_g, post_g, wg, wu, wd)


Q_SCALE = HEAD_DIM ** -0.5


def _qkv_s_body(x_ref, g_ref, w_ref, q_ref, k_ref, v_ref):
    h = _rms(x_ref[...], g_ref[...]).astype(BF16)
    q_ref[...] = (_dot(h, w_ref[:, 0:A_WIDTH]) * Q_SCALE).astype(BF16)
    k_ref[...] = _dot(h, w_ref[:, A_WIDTH:2 * A_WIDTH]).astype(BF16)
    v_ref[...] = _dot(h, w_ref[:, 2 * A_WIDTH:3 * A_WIDTH]).astype(BF16)


def _qkv_sample(x, g, w_qkv, layer):
    rows = x.shape[0]
    tm = min(QKV_TM, rows)
    out = jax.ShapeDtypeStruct((rows, A_WIDTH), BF16)
    spec = pl.BlockSpec((tm, A_WIDTH), lambda i: (i, 0))
    return pl.pallas_call(
        _qkv_s_body,
        grid=(rows // tm,),
        in_specs=[pl.BlockSpec((tm, D_MODEL), lambda i: (i, 0)),
                  _of_layer((1, D_MODEL), layer), _of_layer((D_MODEL, QKV_COLS), layer)],
        out_specs=[spec, spec, spec],
        out_shape=[out, out, out],
        compiler_params=_cparams("parallel"),
        name="qkv_sample",
    )(x, g, w_qkv)


def _kv_tail(group, tm):
    keep = min(A_GROUPS[group][0], SEQ)
    rows = min(keep, tm)
    return (SEQ - keep) // tm, rows, tm - rows


def _qkv_p_body(x_ref, g_ref, w_ref, o1_ref, o2_ref, o3_ref, t1_ref, t2_ref, t3_ref, acc_ref, *, tm):
    h = _rms(x_ref[...], g_ref[...]).astype(BF16)
    res = _dot(h, w_ref[...])
    for c in range(QKV_COLS // LANES):
        acc_ref[c] = res[:, c * LANES:(c + 1) * LANES]
    for g, (_, dil) in enumerate(A_GROUPS):
        out = (o1_ref, o2_ref, o3_ref)[g]
        for c in range(3 * A_OUT // LANES):
            part, sub = divmod(c * LANES, A_OUT)
            src = (part * A_WIDTH + g * A_OUT + sub) // LANES
            for r in range(dil):
                val = acc_ref[src] if dil == 1 else acc_ref[src, pl.ds(r, tm // dil, stride=dil), :]
                if part == 0:
                    val = val * Q_SCALE
                out[0, r, :, c * LANES:(c + 1) * LANES] = val.astype(BF16)
    for g in range(len(A_GROUPS)):
        first, rows, row0 = _kv_tail(g, tm)
        out_t = (t1_ref, t2_ref, t3_ref)[g]

        @pl.when(pl.program_id(1) >= first)
        def _(g=g, rows=rows, row0=row0, out_t=out_t):
            for c in range(2 * A_OUT // LANES):
                part, sub = divmod(c * LANES, A_OUT)
                src = ((part + 1) * A_WIDTH + g * A_OUT + sub) // LANES
                out_t[0, c * LANES:(c + 1) * LANES, :] = acc_ref[src, row0:row0 + rows, :].T


def _qkv_prompt(x, g, w_qkv, layer):
    tm = QKV_TM
    steps = SEQ // tm
    dils = [dil for _, dil in A_GROUPS]
    tails = [_kv_tail(g, tm) for g in range(len(A_GROUPS))]
    tail_spec = lambda first, rows: pl.BlockSpec((1, 2 * A_OUT, rows), lambda b, j: (b, 0, jnp.maximum(j - first, 0)))
    return pl.pallas_call(
        functools.partial(_qkv_p_body, tm=tm),
        grid=(BATCH, steps),
        in_specs=[pl.BlockSpec((tm, D_MODEL), lambda b, j: (b * steps + j, 0)),
                  _of_layer((1, D_MODEL), layer), _of_layer((D_MODEL, QKV_COLS), layer)],
        out_specs=([pl.BlockSpec((1, dil, tm // dil, 3 * A_OUT), lambda b, j: (b, 0, j, 0)) for dil in dils]
                   + [tail_spec(first, rows) for first, rows, _ in tails]),
        out_shape=([jax.ShapeDtypeStruct((BATCH, dil, SEQ // dil, 3 * A_OUT), BF16) for dil in dils]
                   + [jax.ShapeDtypeStruct((BATCH, 2 * A_OUT, min(win, SEQ)), F32) for win, _ in A_GROUPS]),
        scratch_shapes=[pltpu.VMEM((QKV_COLS // LANES, tm, LANES), F32)],
        compiler_params=_cparams("parallel", "arbitrary"),
        name="qkv_prompt",
    )(x, g, w_qkv)


def _alibi_slopes():
    return np.exp2(-8.0 * np.arange(1, A_HEADS + 1, dtype=np.float64) / A_HEADS)


def _prompt_bias(group):
    _, dil = A_GROUPS[group]
    rel = NK + np.arange(NK)[:, None] - np.arange(2 * NK)[None, :]
    valid = (rel >= 0) & (rel <= NK)
    slopes = _alibi_slopes()[group * A_HPG:(group + 1) * A_HPG]
    bias = -slopes[:, None, None] * (dil * rel)[None].astype(np.float64)
    return np.where(valid[None], bias, NEG).astype(np.float32).reshape(A_HPG * NK, 2 * NK)


def _attn_p_body(q_ref, kp_ref, kc_ref, vp_ref, vc_ref, bias_ref, o_ref, lse_ref, *, tq, rb):
    first = (pl.program_id(2) == 0).astype(F32)
    lane_head = lax.broadcasted_iota(jnp.int32, (1, A_OUT), 1) // HEAD_DIM
    key_col = lax.broadcasted_iota(jnp.int32, (1, 2 * NK), 1)
    no_prev = jnp.where(key_col < NK, NEG, 0.0) * first
    blocks = [(r, j) for r in range(rb) for j in range(tq // NK)]
    blk = lambda ref, r, j: ref[0, r, j * NK:(j + 1) * NK, :]
    scores, values = [], []
    for r, j in blocks:
        qj = blk(q_ref, r, j)
        qs = jnp.concatenate([jnp.where(lane_head == h, qj, jnp.zeros_like(qj)) for h in range(A_HPG)], axis=0)
        k_prev, v_prev = ((kp_ref[0, r], vp_ref[0, r]) if j == 0
                          else (blk(kc_ref, r, j - 1), blk(vc_ref, r, j - 1)))
        s = _dot_nt(qs, jnp.concatenate([k_prev, blk(kc_ref, r, j)], axis=0)) + bias_ref[...]
        scores.append(s + no_prev if j == 0 else s)
        values.append(jnp.concatenate([v_prev, blk(vc_ref, r, j)], axis=0))
    stats = []
    for i in range(len(blocks)):
        m = jnp.max(scores[i], axis=-1, keepdims=True)
        p = jnp.exp(scores[i] - m)
        den = jnp.sum(p, axis=-1, keepdims=True)
        scores[i] = p.astype(BF16)
        stats.append((den, m + jnp.log(den)))
    for i, (r, j) in enumerate(blocks):
        den, lse = stats[i]
        pv = _dot(scores[i], values[i]) / den
        o_acc = jnp.zeros((NK, A_OUT), F32)
        l_acc = jnp.zeros((NK, A_OUT), F32)
        for h in range(A_HPG):
            rows = slice(h * NK, (h + 1) * NK)
            o_acc = jnp.where(lane_head == h, pv[rows], o_acc)
            l_acc = jnp.where(lane_head == h, lse[rows], l_acc)
        o_ref[0, r, j * NK:(j + 1) * NK, :] = o_acc
        lse_ref[0, r, j * NK:(j + 1) * NK, :] = l_acc


def _attn_prompt(qkv, group):
    _, dil = A_GROUPS[group]
    sub = SEQ // dil
    tq = min(ATT_TQ, sub)
    rb = min(dil, ATT_TQ // tq)
    cur = lambda col: pl.BlockSpec((1, rb, tq, A_OUT), lambda b, r, i: (b, r, i, col))
    prev = lambda col: pl.BlockSpec((1, rb, NK, A_OUT),
                                    lambda b, r, i: (b, r, jnp.maximum(i * (tq // NK) - 1, 0), col))
    out = jax.ShapeDtypeStruct((BATCH, dil, sub, A_OUT), F32)
    return pl.pallas_call(
        functools.partial(_attn_p_body, tq=tq, rb=rb),
        grid=(BATCH, dil // rb, sub // tq),
        in_specs=[cur(0), prev(1), cur(1), prev(2), cur(2), _resident((A_HPG * NK, 2 * NK))],
        out_specs=[cur(0), cur(0)],
        out_shape=[out, out],
        compiler_params=_cparams("parallel", "parallel", "arbitrary"),
        name=f"attn_prompt_g{group}",
    )(qkv, qkv, qkv, qkv, qkv, jnp.asarray(_prompt_bias(group)))


SAMP_CACHE_LEN = tuple(win for win, _ in A_GROUPS)
SAMP_NEW_SLOT = 128
SAMP_SEG = []
_off = 0
for _len in SAMP_CACHE_LEN:
    SAMP_SEG.append((_off, _off + _len, _off + _len + SAMP_NEW_SLOT))
    _off += _len + SAMP_NEW_SLOT
SAMP_KEYS = _off
SAMP_Q = A_HPG * DEC_SEQ
NEW_PAD = 16


def _sample_bias():
    slopes = _alibi_slopes()
    bias = np.full((SAMP_Q, SAMP_KEYS), NEG, np.float64)
    for c in range(SAMP_Q):
        h, t = divmod(c, DEC_SEQ)
        for g, (win, dil) in enumerate(A_GROUPS):
            c0, n0, _ = SAMP_SEG[g]
            sl = slopes[g * A_HPG + h]
            dist = win + t - np.arange(win)
            ok = (dist % dil == 0) & (dist <= win)
            bias[c, c0:c0 + win] = np.where(ok, -sl * dist, NEG)
            for t2 in range(t + 1):
                if (t - t2) % dil == 0:
                    bias[c, n0 + t2] = -sl * (t - t2)
    return bias.astype(np.float32)


def _attn_s_body(q_ref, k_ref, v_ref, c1_ref, c2_ref, c3_ref, bias_ref, o_ref, knew, vnew, *, sb):
    @pl.when(pl.program_id(0) == 0)
    def _():
        knew[...] = jnp.zeros(knew.shape, BF16)
        vnew[...] = jnp.zeros(vnew.shape, BF16)

    lane_head = lax.broadcasted_iota(jnp.int32, (1, A_OUT), 1) // HEAD_DIM
    pad = jnp.zeros((NEW_PAD - DEC_SEQ, A_OUT), F32)
    caches = (c1_ref, c2_ref, c3_ref)

    def one_seq(n, carry):
        qn, kn, vn = q_ref[n], k_ref[n], v_ref[n]
        scores = []
        for g in range(3):
            c0, n0, e0 = SAMP_SEG[g]
            cols = slice(g * A_OUT, (g + 1) * A_OUT)
            knew[g, 0:NEW_PAD, :] = jnp.concatenate([kn[:, cols], pad], axis=0).astype(BF16)
            vnew[g, 0:NEW_PAD, :] = jnp.concatenate([vn[:, cols], pad], axis=0).astype(BF16)
            qbd = jnp.concatenate([jnp.where(lane_head == h, qn[:, cols], 0.0) for h in range(A_HPG)],
                                  axis=0).astype(BF16)
            k_t = caches[g][0, n, 0:A_OUT, :].astype(BF16)
            scores.append(_dot(qbd, k_t) + bias_ref[:, c0:n0])
            scores.append(_dot_nt(qbd, knew[g]) + bias_ref[:, n0:e0])
        m = functools.reduce(jnp.maximum, [jnp.max(s, axis=-1, keepdims=True) for s in scores])
        den = jnp.zeros((SAMP_Q, 1), F32)
        out = jnp.zeros((SAMP_Q, A_OUT), F32)
        for g in range(3):
            p_c = jnp.exp(scores[2 * g] - m)
            p_n = jnp.exp(scores[2 * g + 1] - m)
            den = den + jnp.sum(p_c, axis=-1, keepdims=True) + jnp.sum(p_n, axis=-1, keepdims=True)
            v_t = caches[g][0, n, A_OUT:2 * A_OUT, :].astype(BF16)
            out = out + _dot_nt(p_c.astype(BF16), v_t) + _dot(p_n.astype(BF16), vnew[g])
        out = out / den
        attn = jnp.zeros((DEC_SEQ, A_OUT), F32)
        for h in range(A_HPG):
            attn = jnp.where(lane_head == h, out[h * DEC_SEQ:(h + 1) * DEC_SEQ, :], attn)
        o_ref[n] = attn
        return carry

    lax.fori_loop(0, sb, one_seq, 0)


def _attn_sample(q, k, v, c1, c2, c3, layer):
    sb = SAMP_SB
    qspec = pl.BlockSpec((sb, DEC_SEQ, A_WIDTH), lambda i: (i, 0, 0))
    cspec = lambda n: pl.BlockSpec((1, sb, 2 * A_OUT, n), lambda i: (layer, i, 0, 0))
    return pl.pallas_call(
        functools.partial(_attn_s_body, sb=sb),
        grid=(DEC_BATCH // sb,),
        in_specs=[qspec, qspec, qspec] + [cspec(n) for n in SAMP_CACHE_LEN] + [_resident((SAMP_Q, SAMP_KEYS))],
        out_specs=pl.BlockSpec((sb, DEC_SEQ, A_OUT), lambda i: (i, 0, 0)),
        out_shape=jax.ShapeDtypeStruct((DEC_BATCH, DEC_SEQ, A_OUT), F32),
        scratch_shapes=[pltpu.VMEM((3, SAMP_NEW_SLOT, A_OUT), BF16), pltpu.VMEM((3, SAMP_NEW_SLOT, A_OUT), BF16)],
        compiler_params=_cparams("arbitrary"),
        name="attn_sample",
    )(q, k, v, c1, c2, c3, jnp.asarray(_sample_bias()))


def _proj(h, w_ref, off, width):
    return _dot(h, w_ref[:, off:off + width])


def _gate(h, w_ref, branch):
    return jax.nn.sigmoid(_proj(h, w_ref, OFF_GATES + branch * D_MODEL, D_MODEL))


_MIX_WEIGHT_SHAPES = (
    (1, D_MODEL), (1, D_MODEL),
    (D_MODEL, REST_COLS),
    (A_OUT, D_MODEL), (B_WIDTH, D_MODEL), (C_WIDTH, D_MODEL), (D_WIDTH, D_MODEL), (D_MODEL, D_MODEL),
    (B_CONV, B_WIDTH), (1, C_WIDTH), (1, C_WIDTH),
    (D_CONV, D_WIDTH), (1, D_WIDTH), (1, D_WIDTH), (1, D_WIDTH),
)


def _mixer_p_body(x_ref, o1_ref, o2_ref, o3_ref, l1_ref, l2_ref, l3_ref,
                  pre_ref, post_ref, w_ref, woa_ref, wob_ref, woc_ref, wod_ref, wo_ref,
                  cbw_ref, glng_ref, glnb_ref, cdw_ref, cdb_ref, dlng_ref, dlnb_ref,
                  gws_ref, gbc_ref,
                  xo_ref, nb_ref, nd_ref, zbuf, gbuf, nat, *, tm):
    @pl.when(pl.program_id(1) == 0)
    def _():
        zbuf[0:ZHALO, :] = jnp.zeros((ZHALO, B_WIDTH), F32)
        gbuf[0:GHALO, :] = jnp.zeros((GHALO, D_WIDTH), F32)

    x = x_ref[...]
    h = _rms(x, pre_ref[...]).astype(BF16)

    gbuf[GHALO:GHALO + tm, :] = (_proj(h, w_ref, OFF_GLUA, D_WIDTH)
                                 * jax.nn.sigmoid(_proj(h, w_ref, OFF_GLUG, D_WIDTH)))
    wrows = MIX_RC + GHALO
    dc = []
    for ci in range(tm // MIX_RC):
        win = gbuf[ci * MIX_RC:ci * MIX_RC + wrows, :]
        phase = [win] + [pltpu.roll(win, wrows - s, axis=0) for s in range(1, SUBLANES)]
        acc = jnp.broadcast_to(cdb_ref[...], (MIX_RC, D_WIDTH))
        for k in range(D_CONV):
            a, s = divmod(GHALO - (D_CONV - 1) + k, SUBLANES)
            acc = acc + cdw_ref[k:k + 1, :] * phase[s][a * SUBLANES:a * SUBLANES + MIX_RC, :]
        dc.append(acc)
    tail = gbuf[tm:tm + GHALO, :]
    nd_ref[0] = tail
    gbuf[0:GHALO, :] = tail

    def natural(ref, slot, dil):
        halves = A_OUT // LANES
        for r in range(dil):
            for c in range(halves):
                nat[slot * halves + c, pl.ds(r, tm // dil, stride=dil), :] = ref[0, r, :, c * LANES:(c + 1) * LANES]
        return jnp.concatenate([nat[slot * halves + c] for c in range(halves)], axis=1)

    dil2, dil3 = A_GROUPS[1][1], A_GROUPS[2][1]
    o1, l1 = o1_ref[...], l1_ref[...]
    o2, l2 = natural(o2_ref, 0, dil2), natural(l2_ref, 1, dil2)
    o3, l3 = natural(o3_ref, 2, dil3), natural(l3_ref, 3, dil3)
    mx = jnp.maximum(jnp.maximum(l1, l2), l3)
    e1, e2, e3 = jnp.exp(l1 - mx), jnp.exp(l2 - mx), jnp.exp(l3 - mx)
    attn = (e1 * o1 + e2 * o2 + e3 * o3) / (e1 + e2 + e3)
    merged = _gate(h, w_ref, 0) * _dot(attn.astype(BF16), woa_ref[...])

    zbuf[ZHALO:ZHALO + tm, :] = _proj(h, w_ref, OFF_CGATE, B_WIDTH) * _proj(h, w_ref, OFF_BIN, B_WIDTH)
    conv = jnp.zeros((tm, B_WIDTH), F32)
    for k in range(B_CONV):
        start = ZHALO - (B_CONV - 1) + k
        conv = conv + cbw_ref[k:k + 1, :] * zbuf[start:start + tm, :]
    br = _dot((_proj(h, w_ref, OFF_BGATE, B_WIDTH) * conv).astype(BF16), wob_ref[...])
    merged = merged + _gate(h, w_ref, 1) * br
    tail = zbuf[tm:tm + ZHALO, :]
    nb_ref[0] = tail
    zbuf[0:ZHALO, :] = tail

    vn = _ln(_proj(h, w_ref, OFF_GV, C_WIDTH), glng_ref[...], glnb_ref[...]).astype(BF16)
    r_i = lax.broadcasted_iota(jnp.int32, (C_CHUNK, C_CHUNK), 0)
    c_i = lax.broadcasted_iota(jnp.int32, (C_CHUNK, C_CHUNK), 1)
    ws = [jnp.where(r_i >= c_i, gws_ref[g], 0.0).astype(BF16) for g in range(C_GROUPS)]
    chunks = []
    for c in range(tm // C_CHUNK):
        rows = slice(c * C_CHUNK, (c + 1) * C_CHUNK)
        chunks.append(jnp.concatenate(
            [_dot(ws[g], vn[rows, g * C_GDIM:(g + 1) * C_GDIM]) + gbc_ref[:, g:g + 1]
             for g in range(C_GROUPS)], axis=1))
    mixed = jnp.concatenate(chunks, axis=0)
    br = _dot((_proj(h, w_ref, OFF_U, C_WIDTH) * mixed).astype(BF16), woc_ref[...])
    merged = merged + _gate(h, w_ref, 2) * br

    dn = _ln(jnp.concatenate(dc, axis=0), dlng_ref[...], dlnb_ref[...])
    br = _dot(jax.nn.silu(dn).astype(BF16), wod_ref[...])
    merged = merged + _gate(h, w_ref, 3) * br

    y = _dot(merged.astype(BF16), wo_ref[...])
    xo_ref[...] = x + _rms(y, post_ref[...])


def _mixer_prompt(x, o, lse, weights, gws, gb_col, layer):
    tm = MIX_TM
    steps = SEQ // tm
    row = lambda width: pl.BlockSpec((tm, width), lambda b, j: (b * steps + j, 0))
    tail = lambda n: pl.BlockSpec((1, n, B_WIDTH), lambda b, j: (b, 0, 0))
    res = lambda dil: pl.BlockSpec((1, dil, tm // dil, A_OUT), lambda b, j: (b, 0, j, 0))
    attn_specs = [row(A_OUT)] + [res(dil) for _, dil in A_GROUPS[1:]]
    flat = lambda a: a.reshape(BATCH * SEQ, A_OUT)
    return pl.pallas_call(
        functools.partial(_mixer_p_body, tm=tm),
        grid=(BATCH, steps),
        in_specs=([row(D_MODEL)] + attn_specs + attn_specs
                  + [_of_layer(s, layer) for s in _MIX_WEIGHT_SHAPES]
                  + [_of_layer((C_GROUPS, C_CHUNK, C_CHUNK), layer), _of_layer((C_CHUNK, C_GROUPS), layer)]),
        out_specs=[row(D_MODEL), tail(ZHALO), tail(GHALO)],
        out_shape=[jax.ShapeDtypeStruct((BATCH * SEQ, D_MODEL), F32),
                   jax.ShapeDtypeStruct((BATCH, ZHALO, B_WIDTH), F32),
                   jax.ShapeDtypeStruct((BATCH, GHALO, D_WIDTH), F32)],
        scratch_shapes=[pltpu.VMEM((tm + ZHALO, B_WIDTH), F32),
                        pltpu.VMEM((tm + GHALO, D_WIDTH), F32),
                        pltpu.VMEM((4 * A_OUT // LANES, tm, LANES), F32)],
        compiler_params=_cparams("parallel", "arbitrary"),
        name="mixer_prompt",
    )(x, flat(o[0]), o[1], o[2], flat(lse[0]), lse[1], lse[2], *weights, gws, gb_col)


def _mixer_s_body(x_ref, attn_ref, sb_ref, sd_ref,
                  pre_ref, post_ref, w_ref, woa_ref, wob_ref, woc_ref, wod_ref, wo_ref,
                  cbw_ref, glng_ref, glnb_ref, cdw_ref, cdb_ref, dlng_ref, dlnb_ref,
                  wtab_ref, btab_ref,
                  xo_ref, nb_ref, vn_ref, glu_ref, *, sn):
    rows = DEC_SEQ * sn
    x = x_ref[...].reshape(rows, D_MODEL)
    h = _rms(x, pre_ref[...]).astype(BF16)
    slab = lambda a, t: a[t * sn:(t + 1) * sn, :]

    merged = _gate(h, w_ref, 0) * _dot(attn_ref[...].reshape(rows, A_OUT).astype(BF16), woa_ref[...])

    zc = _proj(h, w_ref, OFF_CGATE, B_WIDTH) * _proj(h, w_ref, OFF_BIN, B_WIDTH)
    zpad = [sb_ref[i] for i in range(B_CONV - 1)] + [slab(zc, t) for t in range(DEC_SEQ)]
    conv = jnp.concatenate(
        [sum(cbw_ref[k:k + 1, :] * zpad[t + k] for k in range(B_CONV)) for t in range(DEC_SEQ)], axis=0)
    br = _dot((_proj(h, w_ref, OFF_BGATE, B_WIDTH) * conv).astype(BF16), wob_ref[...])
    merged = merged + _gate(h, w_ref, 1) * br
    for i in range(B_CONV - 1):
        nb_ref[i] = zpad[DEC_SEQ + i]

    vn = _ln(_proj(h, w_ref, OFF_GV, C_WIDTH), glng_ref[...], glnb_ref[...])
    vn_ref[...] = vn.reshape(DEC_SEQ, sn, C_WIDTH)
    mixed = []
    for t in range(DEC_SEQ):
        acc = jnp.broadcast_to(btab_ref[t:t + 1, :], (sn, C_WIDTH))
        for s in range(t + 1):
            acc = acc + wtab_ref[t * DEC_SEQ + s:t * DEC_SEQ + s + 1, :] * slab(vn, s)
        mixed.append(acc)
    mixed = jnp.concatenate(mixed, axis=0)
    br = _dot((_proj(h, w_ref, OFF_U, C_WIDTH) * mixed).astype(BF16), woc_ref[...])
    merged = merged + _gate(h, w_ref, 2) * br

    glu = _proj(h, w_ref, OFF_GLUA, D_WIDTH) * jax.nn.sigmoid(_proj(h, w_ref, OFF_GLUG, D_WIDTH))
    glu_ref[...] = glu.reshape(DEC_SEQ, sn, D_WIDTH)
    gpad = [sd_ref[i] for i in range(D_CONV - 1)] + [slab(glu, t) for t in range(DEC_SEQ)]
    dc = []
    for t in range(DEC_SEQ):
        acc = jnp.broadcast_to(cdb_ref[...], (sn, D_WIDTH))
        for k in range(D_CONV):
            acc = acc + cdw_ref[k:k + 1, :] * gpad[t + k]
        dc.append(acc)
    dn = _ln(jnp.concatenate(dc, axis=0), dlng_ref[...], dlnb_ref[...])
    br = _dot(jax.nn.silu(dn).astype(BF16), wod_ref[...])
    merged = merged + _gate(h, w_ref, 3) * br

    y = _dot(merged.astype(BF16), wo_ref[...])
    xo_ref[...] = (x + _rms(y, post_ref[...])).reshape(DEC_SEQ, sn, D_MODEL)


def _mixer_sample(x, attn, state_b, state_d, weights, wtab, btab, layer):
    sn = SAMP_SN
    tslab = lambda n, width: pl.BlockSpec((n, sn, width), lambda i: (0, i, 0))
    return pl.pallas_call(
        functools.partial(_mixer_s_body, sn=sn),
        grid=(DEC_BATCH // sn,),
        in_specs=([tslab(DEC_SEQ, D_MODEL), tslab(DEC_SEQ, A_OUT),
                   tslab(B_CONV - 1, B_WIDTH), tslab(D_CONV - 1, D_WIDTH)]
                  + [_of_layer(s, layer) for s in _MIX_WEIGHT_SHAPES]
                  + [_resident((DEC_SEQ * DEC_SEQ, C_WIDTH)), _resident((DEC_SEQ, C_WIDTH))]),
        out_specs=[tslab(DEC_SEQ, D_MODEL), tslab(B_CONV - 1, B_WIDTH),
                   tslab(DEC_SEQ, C_WIDTH), tslab(DEC_SEQ, D_WIDTH)],
        out_shape=[jax.ShapeDtypeStruct((DEC_SEQ, DEC_BATCH, D_MODEL), F32),
                   jax.ShapeDtypeStruct((B_CONV - 1, DEC_BATCH, B_WIDTH), F32),
                   jax.ShapeDtypeStruct((DEC_SEQ, DEC_BATCH, C_WIDTH), F32),
                   jax.ShapeDtypeStruct((DEC_SEQ, DEC_BATCH, D_WIDTH), F32)],
        compiler_params=_cparams("parallel"),
        name="mixer_sample",
    )(x, attn, state_b, state_d, *weights, wtab, btab)


def kernel(x_prompt, x_sample, cache_attn_w128, cache_attn_w512, cache_attn_w2048, state_conv_b, state_conv_d, ffn1_pre_g, ffn1_post_g, ffn1_w_gate, ffn1_w_up, ffn1_w_down, mix_pre_g, mix_post_g, w_in, w_out_a, conv_b_w, w_out_b, gmlp_ln_g, gmlp_ln_b, gmlp_ws, gmlp_b, w_out_c, conv_d_w, conv_d_b, conv_d_ln_g, conv_d_ln_b, w_out_d, w_o, ffn2_pre_g, ffn2_post_g, ffn2_w_gate, ffn2_w_up, ffn2_w_down):
    bf = lambda a: a.astype(BF16)
    vec = lambda a: a.reshape(DEPTH, 1, -1)

    xp = x_prompt.reshape(BATCH * SEQ, D_MODEL)
    xs = x_sample.transpose(1, 0, 2).reshape(DEC_SEQ * DEC_BATCH, D_MODEL)
    time_minor = lambda c: c.transpose(0, 1, 3, 4, 5, 2).reshape(DEPTH, DEC_BATCH, 2 * A_OUT, c.shape[2])
    c1, c2, c3 = time_minor(cache_attn_w128), time_minor(cache_attn_w512), time_minor(cache_attn_w2048)
    state_b_tm = state_conv_b.transpose(0, 2, 1, 3)
    state_d_tm = state_conv_d.transpose(0, 2, 1, 3)

    ffn1 = (vec(ffn1_pre_g), vec(ffn1_post_g), bf(ffn1_w_gate), bf(ffn1_w_up), bf(ffn1_w_down))
    ffn2 = (vec(ffn2_pre_g), vec(ffn2_post_g), bf(ffn2_w_gate), bf(ffn2_w_up), bf(ffn2_w_down))
    w_qkv, w_rest = bf(w_in[:, :, :QKV_COLS]), bf(w_in[:, :, QKV_COLS:])
    pre = vec(mix_pre_g)
    mix_w = (pre, vec(mix_post_g), w_rest,
             bf(w_out_a), bf(w_out_b), bf(w_out_c), bf(w_out_d), bf(w_o),
             conv_b_w, vec(gmlp_ln_g), vec(gmlp_ln_b),
             conv_d_w, vec(conv_d_b), vec(conv_d_ln_g), vec(conv_d_ln_b))
    gb_col = gmlp_b.transpose(0, 2, 1)

    pa, sa = [[] for _ in A_GROUPS], [[] for _ in A_GROUPS]
    pb, sb, sc, pd, sd = [], [], [], [], []
    for l in range(DEPTH):
        xp = _ffn(xp, *ffn1, l)
        *qkv, kvt1, kvt2, kvt3 = _qkv_prompt(xp, pre, w_qkv, l)
        outs = [_attn_prompt(qkv[g], g) for g in range(len(A_GROUPS))]
        xp, nb, nd = _mixer_prompt(xp, [o for o, _ in outs], [s for _, s in outs], mix_w, gmlp_ws, gb_col, l)
        xp = _ffn(xp, *ffn2, l)
        for g, kvt in enumerate((kvt1, kvt2, kvt3)):
            pa[g].append(kvt)
        pb.append(nb[:, ZHALO - (B_CONV - 1):])
        pd.append(nd[:, GHALO - (D_CONV - 1):])

        xs = _ffn(xs, *ffn1, l)
        qs, ks, vs = _qkv_sample(xs, pre, w_qkv, l)
        seq_major = lambda a: a.astype(F32).reshape(DEC_SEQ, DEC_BATCH, A_WIDTH).transpose(1, 0, 2)
        qs, ks, vs = seq_major(qs), seq_major(ks), seq_major(vs)
        attn_s = _attn_sample(qs, ks, vs, c1, c2, c3, l).transpose(1, 0, 2)
        ws8 = gmlp_ws[l][:, :DEC_SEQ, :DEC_SEQ]
        wtab = jnp.repeat(ws8.transpose(1, 2, 0).reshape(DEC_SEQ * DEC_SEQ, C_GROUPS), C_GDIM, axis=1)
        btab = jnp.repeat(gmlp_b[l][:, :DEC_SEQ].T, C_GDIM, axis=1)
        xs3, nb_s, vn_s, glu_s = _mixer_sample(
            xs.reshape(DEC_SEQ, DEC_BATCH, D_MODEL), attn_s, state_b_tm[l], state_d_tm[l], mix_w, wtab, btab, l)
        xs = _ffn(xs3.reshape(DEC_SEQ * DEC_BATCH, D_MODEL), *ffn2, l)
        k4 = ks.reshape(DEC_BATCH, DEC_SEQ, A_HEADS, HEAD_DIM)
        v4 = vs.reshape(DEC_BATCH, DEC_SEQ, A_HEADS, HEAD_DIM)
        for g in range(len(A_GROUPS)):
            hs = slice(g * A_HPG, (g + 1) * A_HPG)
            sa[g].append(jnp.stack([k4[:, :, hs], v4[:, :, hs]], axis=2))
        sb.append(nb_s.transpose(1, 0, 2))
        sc.append(vn_s.transpose(1, 0, 2))
        sd.append(jnp.concatenate([state_conv_d[l][:, DEC_SEQ:], glu_s.transpose(1, 0, 2)], axis=1))

    st = lambda xs_: jnp.stack(xs_, axis=0)
    new_kv = lambda ts: st(ts).reshape(DEPTH, BATCH, 2, A_HPG, HEAD_DIM, -1).transpose(0, 1, 5, 2, 3, 4)
    return (xp.reshape(BATCH, SEQ, D_MODEL),
            xs.reshape(DEC_SEQ, DEC_BATCH, D_MODEL).transpose(1, 0, 2),
            new_kv(pa[0]), new_kv(pa[1]), new_kv(pa[2]), st(sa[0]), st(sa[1]), st(sa[2]),
            st(pb), st(sb), st(sc), st(pd), st(sd))
```

```python
import functools

import numpy as np
import jax
import jax.numpy as jnp
from jax import lax
from jax.experimental import pallas as pl
from jax.experimental.pallas import tpu as pltpu

D_MODEL = 1024
BATCH = 4
SEQ = 4096
DEPTH = 2
DEC_BATCH = 128
DEC_SEQ = 8
HEAD_DIM = 64
A_GROUPS = ((128, 1), (512, 4), (2048, 16))
A_HPG = 4
A_HEADS = A_HPG * len(A_GROUPS)
A_WIDTH = A_HEADS * HEAD_DIM
A_OUT = A_HPG * HEAD_DIM
NK = 128
B_WIDTH = 512
B_CONV = 3
C_WIDTH = 512
C_CHUNK = 128
C_GROUPS = 4
C_GDIM = 128
D_WIDTH = 512
D_CONV = 31
D_FF = 2816
EPS = 1e-6
QKV_COLS = 3 * A_WIDTH
OFF_BGATE, OFF_CGATE, OFF_BIN, OFF_U, OFF_GV, OFF_GLUA, OFF_GLUG, OFF_GATES = (
    0, 512, 1024, 1536, 2048, 2560, 3072, 3584)
REST_COLS = OFF_GATES + 4 * D_MODEL

F32 = jnp.float32
BF16 = jnp.bfloat16
NEG = -1e30
SUBLANES = 8
LANES = 128
V7X_VMEM_LIMIT_BYTES = 56 * 1024 * 1024

FFN_TM = 512
FFN_FC = 256
QKV_TM = 512
MIX_TM = 512
MIX_RC = 64
ZHALO = 8
GHALO = 32
ATT_TQ = 512
SAMP_SB = 2
SAMP_SN = 32


def _cparams(*sem):
    return pltpu.CompilerParams(dimension_semantics=sem, vmem_limit_bytes=V7X_VMEM_LIMIT_BYTES)


def _resident(shape):
    nd = len(shape)
    return pl.BlockSpec(shape, lambda *_: (0,) * nd, pipeline_mode=pl.Buffered(1))


def _of_layer(shape, layer):
    nd = len(shape)
    return pl.BlockSpec((None,) + tuple(shape), lambda *_: (layer,) + (0,) * nd, pipeline_mode=pl.Buffered(1))


def _dot(a, b):
    return jnp.dot(a, b, preferred_element_type=F32)


def _dot_nt(a, b):
    return lax.dot_general(a, b, (((1,), (1,)), ((), ())), preferred_element_type=F32)


def _rms(x, g):
    return x * lax.rsqrt(jnp.mean(x * x, axis=-1, keepdims=True) + EPS) * g


def _ln(x, g, b):
    mu = jnp.mean(x, axis=-1, keepdims=True)
    xc = x - mu
    var = jnp.mean(xc * xc, axis=-1, keepdims=True)
    return xc * lax.rsqrt(var + EPS) * g + b


def _ffn_body(x_ref, pre_ref, post_ref, wg_ref, wu_ref, wd_ref, *rest, sb):
    if sb:
        q_ref, k_ref, v_ref, c1_ref, c2_ref, c3_ref, bias_ref, o_ref, attn_ref = rest
        caches = (c1_ref, c2_ref, c3_ref)
        staged = [_samp_scores(n, q_ref, k_ref, v_ref, caches, bias_ref) for n in range(sb)]
        staged = [_samp_probs(scores) + (new_v,) for scores, new_v in staged]
    else:
        (o_ref,) = rest
    x = x_ref[...]
    h = _rms(x, pre_ref[...]).astype(BF16)
    acc = jnp.zeros(x.shape, F32)
    chunks = D_FF // FFN_FC
    for c in range(chunks):
        if sb and c == chunks // 2:
            for n, (probs, den, new_v) in enumerate(staged):
                attn_ref[n] = _samp_out(n, probs, den, new_v, caches)
        sl = slice(c * FFN_FC, (c + 1) * FFN_FC)
        g = _dot(h, wg_ref[:, sl])
        u = _dot(h, wu_ref[:, sl])
        acc = acc + _dot((jax.nn.silu(g) * u).astype(BF16), wd_ref[sl, :])
    o_ref[...] = x + 0.5 * _rms(acc, post_ref[...])


def _ffn(x, pre_g, post_g, wg, wu, wd, layer, attn=None):
    rows = x.shape[0]
    tm = min(FFN_TM, rows)
    steps = rows // tm
    in_specs = [pl.BlockSpec((tm, D_MODEL), lambda i: (i, 0)),
                _of_layer((1, D_MODEL), layer), _of_layer((1, D_MODEL), layer),
                _of_layer((D_MODEL, D_FF), layer), _of_layer((D_MODEL, D_FF), layer),
                _of_layer((D_FF, D_MODEL), layer)]
    out_specs = [pl.BlockSpec((tm, D_MODEL), lambda i: (i, 0))]
    out_shape = [jax.ShapeDtypeStruct((rows, D_MODEL), F32)]
    operands = [x, pre_g, post_g, wg, wu, wd]
    sb = 0
    if attn is not None:
        *operands_a, first_seq = attn
        sb = SAMP_SB
        blk0 = first_seq // sb
        qspec = pl.BlockSpec((sb, DEC_SEQ, A_WIDTH), lambda i: (i + blk0, 0, 0))
        cspec = lambda n: pl.BlockSpec((1, sb, 2 * A_OUT, n), lambda i: (layer, i + blk0, 0, 0))
        in_specs += [qspec, qspec, qspec] + [cspec(n) for n in SAMP_CACHE_LEN] + [_resident((SAMP_Q, SAMP_KEYS))]
        out_specs.append(pl.BlockSpec((sb, DEC_SEQ, A_OUT), lambda i: (i, 0, 0)))
        out_shape.append(jax.ShapeDtypeStruct((steps * sb, DEC_SEQ, A_OUT), F32))
        operands += operands_a + [jnp.asarray(_sample_bias())]
    out = pl.pallas_call(
        functools.partial(_ffn_body, sb=sb),
        grid=(steps,),
        in_specs=in_specs,
        out_specs=out_specs,
        out_shape=out_shape,
        compiler_params=_cparams("parallel"),
        name="ffn_attn" if sb else "ffn",
    )(*operands)
    return out if sb else out[0]


Q_SCALE = HEAD_DIM ** -0.5


def _qkv_s_body(x_ref, g_ref, w_ref, q_ref, k_ref, v_ref):
    h = _rms(x_ref[...], g_ref[...]).astype(BF16)
    q_ref[...] = (_dot(h, w_ref[:, 0:A_WIDTH]) * Q_SCALE).astype(BF16)
    k_ref[...] = _dot(h, w_ref[:, A_WIDTH:2 * A_WIDTH]).astype(BF16)
    v_ref[...] = _dot(h, w_ref[:, 2 * A_WIDTH:3 * A_WIDTH]).astype(BF16)


def _qkv_sample(x, g, w_qkv, layer):
    rows = x.shape[0]
    tm = min(QKV_TM, rows)
    out = jax.ShapeDtypeStruct((rows, A_WIDTH), BF16)
    spec = pl.BlockSpec((tm, A_WIDTH), lambda i: (i, 0))
    return pl.pallas_call(
        _qkv_s_body,
        grid=(rows // tm,),
        in_specs=[pl.BlockSpec((tm, D_MODEL), lambda i: (i, 0)),
                  _of_layer((1, D_MODEL), layer), _of_layer((D_MODEL, QKV_COLS), layer)],
        out_specs=[spec, spec, spec],
        out_shape=[out, out, out],
        compiler_params=_cparams("parallel"),
        name="qkv_sample",
    )(x, g, w_qkv)


def _kv_tail(group, tm):
    keep = min(A_GROUPS[group][0], SEQ)
    rows = min(keep, tm)
    return (SEQ - keep) // tm, rows, tm - rows


def _qkv_p_body(x_ref, g_ref, w_ref, o1_ref, o2_ref, o3_ref, t1_ref, t2_ref, t3_ref, acc_ref, *, tm):
    h = _rms(x_ref[...], g_ref[...]).astype(BF16)
    res = _dot(h, w_ref[...])
    for c in range(QKV_COLS // LANES):
        acc_ref[c] = res[:, c * LANES:(c + 1) * LANES]
    for g, (_, dil) in enumerate(A_GROUPS):
        out = (o1_ref, o2_ref, o3_ref)[g]
        for c in range(3 * A_OUT // LANES):
            part, sub = divmod(c * LANES, A_OUT)
            src = (part * A_WIDTH + g * A_OUT + sub) // LANES
            for r in range(dil):
                val = acc_ref[src] if dil == 1 else acc_ref[src, pl.ds(r, tm // dil, stride=dil), :]
                if part == 0:
                    val = val * Q_SCALE
                out[0, r, :, c * LANES:(c + 1) * LANES] = val.astype(BF16)
    for g in range(len(A_GROUPS)):
        first, rows, row0 = _kv_tail(g, tm)
        out_t = (t1_ref, t2_ref, t3_ref)[g]

        @pl.when(pl.program_id(1) >= first)
        def _(g=g, rows=rows, row0=row0, out_t=out_t):
            for c in range(2 * A_OUT // LANES):
                part, sub = divmod(c * LANES, A_OUT)
                src = ((part + 1) * A_WIDTH + g * A_OUT + sub) // LANES
                out_t[0, c * LANES:(c + 1) * LANES, :] = acc_ref[src, row0:row0 + rows, :].T


def _qkv_prompt(x, g, w_qkv, layer):
    tm = QKV_TM
    steps = SEQ // tm
    dils = [dil for _, dil in A_GROUPS]
    tails = [_kv_tail(g, tm) for g in range(len(A_GROUPS))]
    tail_spec = lambda first, rows: pl.BlockSpec((1, 2 * A_OUT, rows), lambda b, j: (b, 0, jnp.maximum(j - first, 0)))
    return pl.pallas_call(
        functools.partial(_qkv_p_body, tm=tm),
        grid=(BATCH, steps),
        in_specs=[pl.BlockSpec((tm, D_MODEL), lambda b, j: (b * steps + j, 0)),
                  _of_layer((1, D_MODEL), layer), _of_layer((D_MODEL, QKV_COLS), layer)],
        out_specs=([pl.BlockSpec((1, dil, tm // dil, 3 * A_OUT), lambda b, j: (b, 0, j, 0)) for dil in dils]
                   + [tail_spec(first, rows) for first, rows, _ in tails]),
        out_shape=([jax.ShapeDtypeStruct((BATCH, dil, SEQ // dil, 3 * A_OUT), BF16) for dil in dils]
                   + [jax.ShapeDtypeStruct((BATCH, 2 * A_OUT, min(win, SEQ)), F32) for win, _ in A_GROUPS]),
        scratch_shapes=[pltpu.VMEM((QKV_COLS // LANES, tm, LANES), F32)],
        compiler_params=_cparams("parallel", "arbitrary"),
        name="qkv_prompt",
    )(x, g, w_qkv)


def _alibi_slopes():
    return np.exp2(-8.0 * np.arange(1, A_HEADS + 1, dtype=np.float64) / A_HEADS)


def _prompt_bias(group):
    _, dil = A_GROUPS[group]
    rel = NK + np.arange(NK)[:, None] - np.arange(2 * NK)[None, :]
    valid = (rel >= 0) & (rel <= NK)
    slopes = _alibi_slopes()[group * A_HPG:(group + 1) * A_HPG]
    bias = -slopes[:, None, None] * (dil * rel)[None].astype(np.float64)
    return np.where(valid[None], bias, NEG).astype(np.float32).reshape(A_HPG * NK, 2 * NK)


def _attn_p_body(q_ref, kp_ref, kc_ref, vp_ref, vc_ref, bias_ref, o_ref, lse_ref, *, tq, rb):
    first = (pl.program_id(2) == 0).astype(F32)
    lane_head = lax.broadcasted_iota(jnp.int32, (1, A_OUT), 1) // HEAD_DIM
    key_col = lax.broadcasted_iota(jnp.int32, (1, 2 * NK), 1)
    no_prev = jnp.where(key_col < NK, NEG, 0.0) * first
    blocks = [(r, j) for r in range(rb) for j in range(tq // NK)]
    blk = lambda ref, r, j: ref[0, r, j * NK:(j + 1) * NK, :]
    scores, values = [], []
    for r, j in blocks:
        qj = blk(q_ref, r, j)
        qs = jnp.concatenate([jnp.where(lane_head == h, qj, jnp.zeros_like(qj)) for h in range(A_HPG)], axis=0)
        k_prev, v_prev = ((kp_ref[0, r], vp_ref[0, r]) if j == 0
                          else (blk(kc_ref, r, j - 1), blk(vc_ref, r, j - 1)))
        s = _dot_nt(qs, jnp.concatenate([k_prev, blk(kc_ref, r, j)], axis=0)) + bias_ref[...]
        scores.append(s + no_prev if j == 0 else s)
        values.append(jnp.concatenate([v_prev, blk(vc_ref, r, j)], axis=0))
    stats = []
    for i in range(len(blocks)):
        m = jnp.max(scores[i], axis=-1, keepdims=True)
        p = jnp.exp(scores[i] - m)
        den = jnp.sum(p, axis=-1, keepdims=True)
        scores[i] = p.astype(BF16)
        stats.append((den, m + jnp.log(den)))
    for i, (r, j) in enumerate(blocks):
        den, lse = stats[i]
        pv = _dot(scores[i], values[i]) / den
        o_acc = jnp.zeros((NK, A_OUT), F32)
        l_acc = jnp.zeros((NK, A_OUT), F32)
        for h in range(A_HPG):
            rows = slice(h * NK, (h + 1) * NK)
            o_acc = jnp.where(lane_head == h, pv[rows], o_acc)
            l_acc = jnp.where(lane_head == h, lse[rows], l_acc)
        o_ref[0, r, j * NK:(j + 1) * NK, :] = o_acc
        lse_ref[0, r, j * NK:(j + 1) * NK, :] = l_acc


def _attn_prompt(qkv, group):
    _, dil = A_GROUPS[group]
    sub = SEQ // dil
    tq = min(ATT_TQ, sub)
    rb = min(dil, ATT_TQ // tq)
    cur = lambda col: pl.BlockSpec((1, rb, tq, A_OUT), lambda b, r, i: (b, r, i, col))
    prev = lambda col: pl.BlockSpec((1, rb, NK, A_OUT),
                                    lambda b, r, i: (b, r, jnp.maximum(i * (tq // NK) - 1, 0), col))
    out = jax.ShapeDtypeStruct((BATCH, dil, sub, A_OUT), F32)
    return pl.pallas_call(
        functools.partial(_attn_p_body, tq=tq, rb=rb),
        grid=(BATCH, dil // rb, sub // tq),
        in_specs=[cur(0), prev(1), cur(1), prev(2), cur(2), _resident((A_HPG * NK, 2 * NK))],
        out_specs=[cur(0), cur(0)],
        out_shape=[out, out],
        compiler_params=_cparams("parallel", "parallel", "arbitrary"),
        name=f"attn_prompt_g{group}",
    )(qkv, qkv, qkv, qkv, qkv, jnp.asarray(_prompt_bias(group)))


SAMP_CACHE_LEN = tuple(win for win, _ in A_GROUPS)
SAMP_NEW_SLOT = 128
SAMP_SEG = []
_off = 0
for _len in SAMP_CACHE_LEN:
    SAMP_SEG.append((_off, _off + _len, _off + _len + SAMP_NEW_SLOT))
    _off += _len + SAMP_NEW_SLOT
SAMP_KEYS = _off
SAMP_Q = A_HPG * DEC_SEQ


def _sample_bias():
    slopes = _alibi_slopes()
    bias = np.full((SAMP_Q, SAMP_KEYS), NEG, np.float64)
    for c in range(SAMP_Q):
        h, t = divmod(c, DEC_SEQ)
        for g, (win, dil) in enumerate(A_GROUPS):
            c0, n0, _ = SAMP_SEG[g]
            sl = slopes[g * A_HPG + h]
            dist = win + t - np.arange(win)
            ok = (dist % dil == 0) & (dist <= win)
            bias[c, c0:c0 + win] = np.where(ok, -sl * dist, NEG)
            for t2 in range(t + 1):
                if (t - t2) % dil == 0:
                    bias[c, n0 + t2] = -sl * (t - t2)
    return bias.astype(np.float32)


def _samp_scores(n, q_ref, k_ref, v_ref, caches, bias_ref):
    lane_head = lax.broadcasted_iota(jnp.int32, (1, A_OUT), 1) // HEAD_DIM
    qn, kn, vn = q_ref[n], k_ref[n], v_ref[n]
    pad = jnp.zeros((SAMP_NEW_SLOT - DEC_SEQ, A_OUT), F32)
    scores, new_v = [], []
    for g in range(len(A_GROUPS)):
        c0, n0, e0 = SAMP_SEG[g]
        cols = slice(g * A_OUT, (g + 1) * A_OUT)
        k_new = jnp.concatenate([kn[:, cols], pad], axis=0).astype(BF16)
        new_v.append(jnp.concatenate([vn[:, cols], pad], axis=0).astype(BF16))
        qbd = jnp.concatenate([jnp.where(lane_head == h, qn[:, cols], 0.0) for h in range(A_HPG)],
                              axis=0).astype(BF16)
        k_t = caches[g][0, n, 0:A_OUT, :].astype(BF16)
        scores.append(_dot(qbd, k_t) + bias_ref[:, c0:n0])
        scores.append(_dot_nt(qbd, k_new) + bias_ref[:, n0:e0])
    return scores, new_v


def _samp_probs(scores):
    m = functools.reduce(jnp.maximum, [jnp.max(s, axis=-1, keepdims=True) for s in scores])
    probs = [jnp.exp(s - m) for s in scores]
    den = functools.reduce(jnp.add, [jnp.sum(p, axis=-1, keepdims=True) for p in probs])
    return [p.astype(BF16) for p in probs], den


def _samp_out(n, probs, den, new_v, caches):
    lane_head = lax.broadcasted_iota(jnp.int32, (1, A_OUT), 1) // HEAD_DIM
    out = jnp.zeros((SAMP_Q, A_OUT), F32)
    for g in range(len(A_GROUPS)):
        v_t = caches[g][0, n, A_OUT:2 * A_OUT, :].astype(BF16)
        out = out + _dot_nt(probs[2 * g], v_t) + _dot(probs[2 * g + 1], new_v[g])
    out = out / den
    attn = jnp.zeros((DEC_SEQ, A_OUT), F32)
    for h in range(A_HPG):
        attn = jnp.where(lane_head == h, out[h * DEC_SEQ:(h + 1) * DEC_SEQ, :], attn)
    return attn


def _proj(h, w_ref, off, width):
    return _dot(h, w_ref[:, off:off + width])


def _gate(h, w_ref, branch):
    return jax.nn.sigmoid(_proj(h, w_ref, OFF_GATES + branch * D_MODEL, D_MODEL))


_MIX_WEIGHT_SHAPES = (
    (1, D_MODEL), (1, D_MODEL),
    (D_MODEL, REST_COLS),
    (A_OUT, D_MODEL), (B_WIDTH, D_MODEL), (C_WIDTH, D_MODEL), (D_WIDTH, D_MODEL), (D_MODEL, D_MODEL),
    (B_CONV, B_WIDTH), (1, C_WIDTH), (1, C_WIDTH),
    (D_CONV, D_WIDTH), (1, D_WIDTH), (1, D_WIDTH), (1, D_WIDTH),
)


def _mixer_p_body(x_ref, o1_ref, o2_ref, o3_ref, l1_ref, l2_ref, l3_ref,
                  pre_ref, post_ref, w_ref, woa_ref, wob_ref, woc_ref, wod_ref, wo_ref,
                  cbw_ref, glng_ref, glnb_ref, cdw_ref, cdb_ref, dlng_ref, dlnb_ref,
                  gws_ref, gbc_ref,
                  xo_ref, nb_ref, nd_ref, zbuf, gbuf, nat, *, tm):
    @pl.when(pl.program_id(1) == 0)
    def _():
        zbuf[0:ZHALO, :] = jnp.zeros((ZHALO, B_WIDTH), F32)
        gbuf[0:GHALO, :] = jnp.zeros((GHALO, D_WIDTH), F32)

    x = x_ref[...]
    h = _rms(x, pre_ref[...]).astype(BF16)

    gbuf[GHALO:GHALO + tm, :] = (_proj(h, w_ref, OFF_GLUA, D_WIDTH)
                                 * jax.nn.sigmoid(_proj(h, w_ref, OFF_GLUG, D_WIDTH)))
    wrows = MIX_RC + GHALO
    dc = []
    for ci in range(tm // MIX_RC):
        win = gbuf[ci * MIX_RC:ci * MIX_RC + wrows, :]
        phase = [win] + [pltpu.roll(win, wrows - s, axis=0) for s in range(1, SUBLANES)]
        acc = jnp.broadcast_to(cdb_ref[...], (MIX_RC, D_WIDTH))
        for k in range(D_CONV):
            a, s = divmod(GHALO - (D_CONV - 1) + k, SUBLANES)
            acc = acc + cdw_ref[k:k + 1, :] * phase[s][a * SUBLANES:a * SUBLANES + MIX_RC, :]
        dc.append(acc)
    tail = gbuf[tm:tm + GHALO, :]
    nd_ref[0] = tail
    gbuf[0:GHALO, :] = tail

    def natural(ref, slot, dil):
        halves = A_OUT // LANES
        for r in range(dil):
            for c in range(halves):
                nat[slot * halves + c, pl.ds(r, tm // dil, stride=dil), :] = ref[0, r, :, c * LANES:(c + 1) * LANES]
        return jnp.concatenate([nat[slot * halves + c] for c in range(halves)], axis=1)

    dil2, dil3 = A_GROUPS[1][1], A_GROUPS[2][1]
    o1, l1 = o1_ref[...], l1_ref[...]
    o2, l2 = natural(o2_ref, 0, dil2), natural(l2_ref, 1, dil2)
    o3, l3 = natural(o3_ref, 2, dil3), natural(l3_ref, 3, dil3)
    mx = jnp.maximum(jnp.maximum(l1, l2), l3)
    e1, e2, e3 = jnp.exp(l1 - mx), jnp.exp(l2 - mx), jnp.exp(l3 - mx)
    attn = (e1 * o1 + e2 * o2 + e3 * o3) / (e1 + e2 + e3)
    merged = _gate(h, w_ref, 0) * _dot(attn.astype(BF16), woa_ref[...])

    zbuf[ZHALO:ZHALO + tm, :] = _proj(h, w_ref, OFF_CGATE, B_WIDTH) * _proj(h, w_ref, OFF_BIN, B_WIDTH)
    conv = jnp.zeros((tm, B_WIDTH), F32)
    for k in range(B_CONV):
        start = ZHALO - (B_CONV - 1) + k
        conv = conv + cbw_ref[k:k + 1, :] * zbuf[start:start + tm, :]
    br = _dot((_proj(h, w_ref, OFF_BGATE, B_WIDTH) * conv).astype(BF16), wob_ref[...])
    merged = merged + _gate(h, w_ref, 1) * br
    tail = zbuf[tm:tm + ZHALO, :]
    nb_ref[0] = tail
    zbuf[0:ZHALO, :] = tail

    vn = _ln(_proj(h, w_ref, OFF_GV, C_WIDTH), glng_ref[...], glnb_ref[...]).astype(BF16)
    r_i = lax.broadcasted_iota(jnp.int32, (C_CHUNK, C_CHUNK), 0)
    c_i = lax.broadcasted_iota(jnp.int32, (C_CHUNK, C_CHUNK), 1)
    ws = [jnp.where(r_i >= c_i, gws_ref[g], 0.0).astype(BF16) for g in range(C_GROUPS)]
    chunks = []
    for c in range(tm // C_CHUNK):
        rows = slice(c * C_CHUNK, (c + 1) * C_CHUNK)
        chunks.append(jnp.concatenate(
            [_dot(ws[g], vn[rows, g * C_GDIM:(g + 1) * C_GDIM]) + gbc_ref[:, g:g + 1]
             for g in range(C_GROUPS)], axis=1))
    mixed = jnp.concatenate(chunks, axis=0)
    br = _dot((_proj(h, w_ref, OFF_U, C_WIDTH) * mixed).astype(BF16), woc_ref[...])
    merged = merged + _gate(h, w_ref, 2) * br

    dn = _ln(jnp.concatenate(dc, axis=0), dlng_ref[...], dlnb_ref[...])
    br = _dot(jax.nn.silu(dn).astype(BF16), wod_ref[...])
    merged = merged + _gate(h, w_ref, 3) * br

    y = _dot(merged.astype(BF16), wo_ref[...])
    xo_ref[...] = x + _rms(y, post_ref[...])


def _mixer_prompt(x, o, lse, weights, gws, gb_col, layer):
    tm = MIX_TM
    steps = SEQ // tm
    row = lambda width: pl.BlockSpec((tm, width), lambda b, j: (b * steps + j, 0))
    tail = lambda n: pl.BlockSpec((1, n, B_WIDTH), lambda b, j: (b, 0, 0))
    res = lambda dil: pl.BlockSpec((1, dil, tm // dil, A_OUT), lambda b, j: (b, 0, j, 0))
    attn_specs = [row(A_OUT)] + [res(dil) for _, dil in A_GROUPS[1:]]
    flat = lambda a: a.reshape(BATCH * SEQ, A_OUT)
    return pl.pallas_call(
        functools.partial(_mixer_p_body, tm=tm),
        grid=(BATCH, steps),
        in_specs=([row(D_MODEL)] + attn_specs + attn_specs
                  + [_of_layer(s, layer) for s in _MIX_WEIGHT_SHAPES]
                  + [_of_layer((C_GROUPS, C_CHUNK, C_CHUNK), layer), _of_layer((C_CHUNK, C_GROUPS), layer)]),
        out_specs=[row(D_MODEL), tail(ZHALO), tail(GHALO)],
        out_shape=[jax.ShapeDtypeStruct((BATCH * SEQ, D_MODEL), F32),
                   jax.ShapeDtypeStruct((BATCH, ZHALO, B_WIDTH), F32),
                   jax.ShapeDtypeStruct((BATCH, GHALO, D_WIDTH), F32)],
        scratch_shapes=[pltpu.VMEM((tm + ZHALO, B_WIDTH), F32),
                        pltpu.VMEM((tm + GHALO, D_WIDTH), F32),
                        pltpu.VMEM((4 * A_OUT // LANES, tm, LANES), F32)],
        compiler_params=_cparams("parallel", "arbitrary"),
        name="mixer_prompt",
    )(x, flat(o[0]), o[1], o[2], flat(lse[0]), lse[1], lse[2], *weights, gws, gb_col)


def _mixer_s_body(x_ref, attn_ref, sb_ref, sd_ref,
                  pre_ref, post_ref, w_ref, woa_ref, wob_ref, woc_ref, wod_ref, wo_ref,
                  cbw_ref, glng_ref, glnb_ref, cdw_ref, cdb_ref, dlng_ref, dlnb_ref,
                  wtab_ref, btab_ref,
                  xo_ref, nb_ref, vn_ref, glu_ref, *, sn):
    rows = DEC_SEQ * sn
    x = x_ref[...].reshape(rows, D_MODEL)
    h = _rms(x, pre_ref[...]).astype(BF16)
    slab = lambda a, t: a[t * sn:(t + 1) * sn, :]

    merged = _gate(h, w_ref, 0) * _dot(attn_ref[...].reshape(rows, A_OUT).astype(BF16), woa_ref[...])

    zc = _proj(h, w_ref, OFF_CGATE, B_WIDTH) * _proj(h, w_ref, OFF_BIN, B_WIDTH)
    zpad = [sb_ref[i] for i in range(B_CONV - 1)] + [slab(zc, t) for t in range(DEC_SEQ)]
    conv = jnp.concatenate(
        [sum(cbw_ref[k:k + 1, :] * zpad[t + k] for k in range(B_CONV)) for t in range(DEC_SEQ)], axis=0)
    br = _dot((_proj(h, w_ref, OFF_BGATE, B_WIDTH) * conv).astype(BF16), wob_ref[...])
    merged = merged + _gate(h, w_ref, 1) * br
    for i in range(B_CONV - 1):
        nb_ref[i] = zpad[DEC_SEQ + i]

    vn = _ln(_proj(h, w_ref, OFF_GV, C_WIDTH), glng_ref[...], glnb_ref[...])
    vn_ref[...] = vn.reshape(DEC_SEQ, sn, C_WIDTH)
    mixed = []
    for t in range(DEC_SEQ):
        acc = jnp.broadcast_to(btab_ref[t:t + 1, :], (sn, C_WIDTH))
        for s in range(t + 1):
            acc = acc + wtab_ref[t * DEC_SEQ + s:t * DEC_SEQ + s + 1, :] * slab(vn, s)
        mixed.append(acc)
    mixed = jnp.concatenate(mixed, axis=0)
    br = _dot((_proj(h, w_ref, OFF_U, C_WIDTH) * mixed).astype(BF16), woc_ref[...])
    merged = merged + _gate(h, w_ref, 2) * br

    glu = _proj(h, w_ref, OFF_GLUA, D_WIDTH) * jax.nn.sigmoid(_proj(h, w_ref, OFF_GLUG, D_WIDTH))
    glu_ref[...] = glu.reshape(DEC_SEQ, sn, D_WIDTH)
    gpad = [sd_ref[i] for i in range(D_CONV - 1)] + [slab(glu, t) for t in range(DEC_SEQ)]
    dc = []
    for t in range(DEC_SEQ):
        acc = jnp.broadcast_to(cdb_ref[...], (sn, D_WIDTH))
        for k in range(D_CONV):
            acc = acc + cdw_ref[k:k + 1, :] * gpad[t + k]
        dc.append(acc)
    dn = _ln(jnp.concatenate(dc, axis=0), dlng_ref[...], dlnb_ref[...])
    br = _dot(jax.nn.silu(dn).astype(BF16), wod_ref[...])
    merged = merged + _gate(h, w_ref, 3) * br

    y = _dot(merged.astype(BF16), wo_ref[...])
    xo_ref[...] = (x + _rms(y, post_ref[...])).reshape(DEC_SEQ, sn, D_MODEL)


def _mixer_sample(x, attn, state_b, state_d, weights, wtab, btab, layer):
    sn = SAMP_SN
    tslab = lambda n, width: pl.BlockSpec((n, sn, width), lambda i: (0, i, 0))
    return pl.pallas_call(
        functools.partial(_mixer_s_body, sn=sn),
        grid=(DEC_BATCH // sn,),
        in_specs=([tslab(DEC_SEQ, D_MODEL), tslab(DEC_SEQ, A_OUT),
                   tslab(B_CONV - 1, B_WIDTH), tslab(D_CONV - 1, D_WIDTH)]
                  + [_of_layer(s, layer) for s in _MIX_WEIGHT_SHAPES]
                  + [_resident((DEC_SEQ * DEC_SEQ, C_WIDTH)), _resident((DEC_SEQ, C_WIDTH))]),
        out_specs=[tslab(DEC_SEQ, D_MODEL), tslab(B_CONV - 1, B_WIDTH),
                   tslab(DEC_SEQ, C_WIDTH), tslab(DEC_SEQ, D_WIDTH)],
        out_shape=[jax.ShapeDtypeStruct((DEC_SEQ, DEC_BATCH, D_MODEL), F32),
                   jax.ShapeDtypeStruct((B_CONV - 1, DEC_BATCH, B_WIDTH), F32),
                   jax.ShapeDtypeStruct((DEC_SEQ, DEC_BATCH, C_WIDTH), F32),
                   jax.ShapeDtypeStruct((DEC_SEQ, DEC_BATCH, D_WIDTH), F32)],
        compiler_params=_cparams("parallel"),
        name="mixer_sample",
    )(x, attn, state_b, state_d, *weights, wtab, btab)


def kernel(x_prompt, x_sample, cache_attn_w128, cache_attn_w512, cache_attn_w2048, state_conv_b, state_conv_d, ffn1_pre_g, ffn1_post_g, ffn1_w_gate, ffn1_w_up, ffn1_w_down, mix_pre_g, mix_post_g, w_in, w_out_a, conv_b_w, w_out_b, gmlp_ln_g, gmlp_ln_b, gmlp_ws, gmlp_b, w_out_c, conv_d_w, conv_d_b, conv_d_ln_g, conv_d_ln_b, w_out_d, w_o, ffn2_pre_g, ffn2_post_g, ffn2_w_gate, ffn2_w_up, ffn2_w_down):
    bf = lambda a: a.astype(BF16)
    vec = lambda a: a.reshape(DEPTH, 1, -1)

    xp = x_prompt.reshape(BATCH * SEQ, D_MODEL)
    xs = x_sample.transpose(1, 0, 2).reshape(DEC_SEQ * DEC_BATCH, D_MODEL)
    time_minor = lambda c: c.transpose(0, 1, 3, 4, 5, 2).reshape(DEPTH, DEC_BATCH, 2 * A_OUT, c.shape[2])
    c1, c2, c3 = time_minor(cache_attn_w128), time_minor(cache_attn_w512), time_minor(cache_attn_w2048)
    state_b_tm = state_conv_b.transpose(0, 2, 1, 3)
    state_d_tm = state_conv_d.transpose(0, 2, 1, 3)

    ffn1 = (vec(ffn1_pre_g), vec(ffn1_post_g), bf(ffn1_w_gate), bf(ffn1_w_up), bf(ffn1_w_down))
    ffn2 = (vec(ffn2_pre_g), vec(ffn2_post_g), bf(ffn2_w_gate), bf(ffn2_w_up), bf(ffn2_w_down))
    w_qkv, w_rest = bf(w_in[:, :, :QKV_COLS]), bf(w_in[:, :, QKV_COLS:])
    pre = vec(mix_pre_g)
    mix_w = (pre, vec(mix_post_g), w_rest,
             bf(w_out_a), bf(w_out_b), bf(w_out_c), bf(w_out_d), bf(w_o),
             conv_b_w, vec(gmlp_ln_g), vec(gmlp_ln_b),
             conv_d_w, vec(conv_d_b), vec(conv_d_ln_g), vec(conv_d_ln_b))
    gb_col = gmlp_b.transpose(0, 2, 1)

    pa, sa = [[] for _ in A_GROUPS], [[] for _ in A_GROUPS]
    pb, sb, sc, pd, sd = [], [], [], [], []
    for l in range(DEPTH):
        xs = _ffn(xs, *ffn1, l)
        qs, ks, vs = _qkv_sample(xs, pre, w_qkv, l)
        seq_major = lambda a: a.astype(F32).reshape(DEC_SEQ, DEC_BATCH, A_WIDTH).transpose(1, 0, 2)
        qs, ks, vs = seq_major(qs), seq_major(ks), seq_major(vs)
        half = DEC_BATCH // 2

        xp, attn_lo = _ffn(xp, *ffn1, l, attn=(qs, ks, vs, c1, c2, c3, 0))
        *qkv, kvt1, kvt2, kvt3 = _qkv_prompt(xp, pre, w_qkv, l)
        outs = [_attn_prompt(qkv[g], g) for g in range(len(A_GROUPS))]
        xp, nb, nd = _mixer_prompt(xp, [o for o, _ in outs], [s for _, s in outs], mix_w, gmlp_ws, gb_col, l)
        xp, attn_hi = _ffn(xp, *ffn2, l, attn=(qs, ks, vs, c1, c2, c3, half))
        for g, kvt in enumerate((kvt1, kvt2, kvt3)):
            pa[g].append(kvt)
        pb.append(nb[:, ZHALO - (B_CONV - 1):])
        pd.append(nd[:, GHALO - (D_CONV - 1):])

        attn_s = jnp.concatenate([attn_lo, attn_hi], axis=0).transpose(1, 0, 2)
        ws8 = gmlp_ws[l][:, :DEC_SEQ, :DEC_SEQ]
        wtab = jnp.repeat(ws8.transpose(1, 2, 0).reshape(DEC_SEQ * DEC_SEQ, C_GROUPS), C_GDIM, axis=1)
        btab = jnp.repeat(gmlp_b[l][:, :DEC_SEQ].T, C_GDIM, axis=1)
        xs3, nb_s, vn_s, glu_s = _mixer_sample(
            xs.reshape(DEC_SEQ, DEC_BATCH, D_MODEL), attn_s, state_b_tm[l], state_d_tm[l], mix_w, wtab, btab, l)
        xs = _ffn(xs3.reshape(DEC_SEQ * DEC_BATCH, D_MODEL), *ffn2, l)
        k4 = ks.reshape(DEC_BATCH, DEC_SEQ, A_HEADS, HEAD_DIM)
        v4 = vs.reshape(DEC_BATCH, DEC_SEQ, A_HEADS, HEAD_DIM)
        for g in range(len(A_GROUPS)):
            hs = slice(g * A_HPG, (g + 1) * A_HPG)
            sa[g].append(jnp.stack([k4[:, :, hs], v4[:, :, hs]], axis=2))
        sb.append(nb_s.transpose(1, 0, 2))
        sc.append(vn_s.transpose(1, 0, 2))
        sd.append(jnp.concatenate([state_conv_d[l][:, DEC_SEQ:], glu_s.transpose(1, 0, 2)], axis=1))

    st = lambda xs_: jnp.stack(xs_, axis=0)
    new_kv = lambda ts: st(ts).reshape(DEPTH, BATCH, 2, A_HPG, HEAD_DIM, -1).transpose(0, 1, 5, 2, 3, 4)
    return (xp.reshape(BATCH, SEQ, D_MODEL),
            xs.reshape(DEC_SEQ, DEC_BATCH, D_MODEL).transpose(1, 0, 2),
            new_kv(pa[0]), new_kv(pa[1]), new_kv(pa[2]), st(sa[0]), st(sa[1]), st(sa[2]),
            st(pb), st(sb), st(sc), st(pd), st(sd))
```

```python
import functools

import numpy as np
import jax
import jax.numpy as jnp
from jax import lax
from jax.experimental import pallas as pl
from jax.experimental.pallas import tpu as pltpu

D_MODEL = 1024
BATCH = 4
SEQ = 4096
DEPTH = 2
DEC_BATCH = 128
DEC_SEQ = 8
HEAD_DIM = 64
A_GROUPS = ((128, 1), (512, 4), (2048, 16))
A_HPG = 4
A_HEADS = A_HPG * len(A_GROUPS)
A_WIDTH = A_HEADS * HEAD_DIM
A_OUT = A_HPG * HEAD_DIM
NK = 128
B_WIDTH = 512
B_CONV = 3
C_WIDTH = 512
C_CHUNK = 128
C_GROUPS = 4
C_GDIM = 128
D_WIDTH = 512
D_CONV = 31
D_FF = 2816
EPS = 1e-6
QKV_COLS = 3 * A_WIDTH
OFF_BGATE, OFF_CGATE, OFF_BIN, OFF_U, OFF_GV, OFF_GLUA, OFF_GLUG, OFF_GATES = (
    0, 512, 1024, 1536, 2048, 2560, 3072, 3584)
REST_COLS = OFF_GATES + 4 * D_MODEL

F32 = jnp.float32
BF16 = jnp.bfloat16
NEG = -1e30
SUBLANES = 8
LANES = 128
V7X_VMEM_LIMIT_BYTES = 56 * 1024 * 1024

FFN_TM = 512
FFN_FC = 256
QKV_TM = 512
MIX_TM = 512
MIX_RC = 64
ZHALO = 8
GHALO = 32
ATT_TQ = 1024
SAMP_SB = 2
SAMP_SN = 32


def _cparams(*sem, flags=None):
    return pltpu.CompilerParams(dimension_semantics=sem, vmem_limit_bytes=V7X_VMEM_LIMIT_BYTES, flags=flags)


def _resident(shape):
    nd = len(shape)
    return pl.BlockSpec(shape, lambda *_: (0,) * nd, pipeline_mode=pl.Buffered(1))


def _of_layer(shape, layer):
    nd = len(shape)
    return pl.BlockSpec((None,) + tuple(shape), lambda *_: (layer,) + (0,) * nd, pipeline_mode=pl.Buffered(1))


def _dot(a, b):
    return jnp.dot(a, b, preferred_element_type=F32)


def _dot_nt(a, b):
    return lax.dot_general(a, b, (((1,), (1,)), ((), ())), preferred_element_type=F32)


def _rms(x, g):
    return x * lax.rsqrt(jnp.mean(x * x, axis=-1, keepdims=True) + EPS) * g


def _ln(x, g, b):
    mu = jnp.mean(x, axis=-1, keepdims=True)
    xc = x - mu
    var = jnp.mean(xc * xc, axis=-1, keepdims=True)
    return xc * lax.rsqrt(var + EPS) * g + b


def _ffn_body(x_ref, pre_ref, post_ref, wg_ref, wu_ref, wd_ref, *rest, sb):
    if sb:
        q_ref, k_ref, v_ref, c1_ref, c2_ref, c3_ref, bias_ref, o_ref, attn_ref = rest
        caches = (c1_ref, c2_ref, c3_ref)
        staged = [_samp_scores(n, q_ref, k_ref, v_ref, caches, bias_ref) for n in range(sb)]
        staged = [_samp_probs(scores) + (new_v,) for scores, new_v in staged]
    else:
        (o_ref,) = rest
    x = x_ref[...]
    h = _rms(x, pre_ref[...]).astype(BF16)
    acc = jnp.zeros(x.shape, F32)
    chunks = D_FF // FFN_FC
    for c in range(chunks):
        if sb and c == chunks // 2:
            for n, (probs, den, new_v) in enumerate(staged):
                attn_ref[n] = _samp_out(n, probs, den, new_v, caches)
        sl = slice(c * FFN_FC, (c + 1) * FFN_FC)
        g = _dot(h, wg_ref[:, sl])
        u = _dot(h, wu_ref[:, sl])
        acc = acc + _dot((jax.nn.silu(g) * u).astype(BF16), wd_ref[sl, :])
    o_ref[...] = x + 0.5 * _rms(acc, post_ref[...])


def _ffn(x, pre_g, post_g, wg, wu, wd, layer, attn=None):
    rows = x.shape[0]
    tm = min(FFN_TM, rows)
    steps = rows // tm
    in_specs = [pl.BlockSpec((tm, D_MODEL), lambda i: (i, 0)),
                _of_layer((1, D_MODEL), layer), _of_layer((1, D_MODEL), layer),
                _of_layer((D_MODEL, D_FF), layer), _of_layer((D_MODEL, D_FF), layer),
                _of_layer((D_FF, D_MODEL), layer)]
    out_specs = [pl.BlockSpec((tm, D_MODEL), lambda i: (i, 0))]
    out_shape = [jax.ShapeDtypeStruct((rows, D_MODEL), F32)]
    operands = [x, pre_g, post_g, wg, wu, wd]
    sb = 0
    if attn is not None:
        *operands_a, first_seq = attn
        sb = SAMP_SB
        blk0 = first_seq // sb
        qspec = pl.BlockSpec((sb, DEC_SEQ, A_WIDTH), lambda i: (i + blk0, 0, 0))
        cspec = lambda n: pl.BlockSpec((1, sb, 2 * A_OUT, n), lambda i: (layer, i + blk0, 0, 0))
        in_specs += [qspec, qspec, qspec] + [cspec(n) for n in SAMP_CACHE_LEN] + [_resident((SAMP_Q, SAMP_KEYS))]
        out_specs.append(pl.BlockSpec((sb, DEC_SEQ, A_OUT), lambda i: (i, 0, 0)))
        out_shape.append(jax.ShapeDtypeStruct((steps * sb, DEC_SEQ, A_OUT), F32))
        operands += operands_a + [jnp.asarray(_sample_bias())]
    out = pl.pallas_call(
        functools.partial(_ffn_body, sb=sb),
        grid=(steps,),
        in_specs=in_specs,
        out_specs=out_specs,
        out_shape=out_shape,
        compiler_params=_cparams("parallel"),
        name="ffn_attn" if sb else "ffn",
    )(*operands)
    return out if sb else out[0]


Q_SCALE = HEAD_DIM ** -0.5


def _qkv_s_body(x_ref, g_ref, w_ref, q_ref, k_ref, v_ref):
    h = _rms(x_ref[...], g_ref[...]).astype(BF16)
    q_ref[...] = (_dot(h, w_ref[:, 0:A_WIDTH]) * Q_SCALE).astype(BF16)
    k_ref[...] = _dot(h, w_ref[:, A_WIDTH:2 * A_WIDTH]).astype(BF16)
    v_ref[...] = _dot(h, w_ref[:, 2 * A_WIDTH:3 * A_WIDTH]).astype(BF16)


def _qkv_sample(x, g, w_qkv, layer):
    rows = x.shape[0]
    tm = min(QKV_TM, rows)
    out = jax.ShapeDtypeStruct((rows, A_WIDTH), BF16)
    spec = pl.BlockSpec((tm, A_WIDTH), lambda i: (i, 0))
    return pl.pallas_call(
        _qkv_s_body,
        grid=(rows // tm,),
        in_specs=[pl.BlockSpec((tm, D_MODEL), lambda i: (i, 0)),
                  _of_layer((1, D_MODEL), layer), _of_layer((D_MODEL, QKV_COLS), layer)],
        out_specs=[spec, spec, spec],
        out_shape=[out, out, out],
        compiler_params=_cparams("parallel"),
        name="qkv_sample",
    )(x, g, w_qkv)


def _kv_tail(group, tm):
    keep = min(A_GROUPS[group][0], SEQ)
    rows = min(keep, tm)
    return (SEQ - keep) // tm, rows, tm - rows


def _qkv_p_body(x_ref, g_ref, w_ref, o1_ref, o2_ref, o3_ref, t1_ref, t2_ref, t3_ref, acc_ref, *, tm):
    h = _rms(x_ref[...], g_ref[...]).astype(BF16)
    for g, (_, dil) in enumerate(A_GROUPS):
        for part in range(3):
            src = part * A_WIDTH + g * A_OUT
            res = _dot(h, w_ref[:, src:src + A_OUT])
            for sub in range(A_OUT // LANES):
                acc_ref[src // LANES + sub] = res[:, sub * LANES:(sub + 1) * LANES]
        out = (o1_ref, o2_ref, o3_ref)[g]
        for c in range(3 * A_OUT // LANES):
            part, sub = divmod(c * LANES, A_OUT)
            src = (part * A_WIDTH + g * A_OUT + sub) // LANES
            for r in range(dil):
                val = acc_ref[src] if dil == 1 else acc_ref[src, pl.ds(r, tm // dil, stride=dil), :]
                if part == 0:
                    val = val * Q_SCALE
                out[0, r, :, c * LANES:(c + 1) * LANES] = val.astype(BF16)
        _, rows, row0 = _kv_tail(g, tm)
        out_t = (t1_ref, t2_ref, t3_ref)[g]
        for c in range(2 * A_OUT // LANES):
            part, sub = divmod(c * LANES, A_OUT)
            src = ((part + 1) * A_WIDTH + g * A_OUT + sub) // LANES
            out_t[0, c * LANES:(c + 1) * LANES, :] = acc_ref[src, row0:row0 + rows, :].T


def _qkv_prompt(x, g, w_qkv, layer):
    tm = QKV_TM
    steps = SEQ // tm
    dils = [dil for _, dil in A_GROUPS]
    tails = [_kv_tail(g, tm) for g in range(len(A_GROUPS))]
    tail_spec = lambda first, rows: pl.BlockSpec((1, 2 * A_OUT, rows), lambda b, j: (b, 0, jnp.maximum(j - first, 0)))
    return pl.pallas_call(
        functools.partial(_qkv_p_body, tm=tm),
        grid=(BATCH, steps),
        in_specs=[pl.BlockSpec((tm, D_MODEL), lambda b, j: (b * steps + j, 0)),
                  _of_layer((1, D_MODEL), layer), _of_layer((D_MODEL, QKV_COLS), layer)],
        out_specs=([pl.BlockSpec((1, dil, tm // dil, 3 * A_OUT), lambda b, j: (b, 0, j, 0)) for dil in dils]
                   + [tail_spec(first, rows) for first, rows, _ in tails]),
        out_shape=([jax.ShapeDtypeStruct((BATCH, dil, SEQ // dil, 3 * A_OUT), BF16) for dil in dils]
                   + [jax.ShapeDtypeStruct((BATCH, 2 * A_OUT, min(win, SEQ)), F32) for win, _ in A_GROUPS]),
        scratch_shapes=[pltpu.VMEM((QKV_COLS // LANES, tm, LANES), F32)],
        compiler_params=_cparams("parallel", "arbitrary"),
        name="qkv_prompt",
    )(x, g, w_qkv)


def _alibi_slopes():
    return np.exp2(-8.0 * np.arange(1, A_HEADS + 1, dtype=np.float64) / A_HEADS)


def _prompt_bias(group):
    _, dil = A_GROUPS[group]
    rel = NK + np.arange(NK)[:, None] - np.arange(2 * NK)[None, :]
    valid = (rel >= 0) & (rel <= NK)
    slopes = _alibi_slopes()[group * A_HPG:(group + 1) * A_HPG]
    bias = -slopes[:, None, None] * (dil * rel)[None].astype(np.float64)
    return np.where(valid[None], bias, NEG).astype(np.float32).reshape(A_HPG * NK, 2 * NK)


def _attn_p_body(q_ref, kp_ref, kc_ref, vp_ref, vc_ref, bias_ref, o_ref, lse_ref, *, tq, rb):
    first = (pl.program_id(2) == 0).astype(F32)
    lane_head = lax.broadcasted_iota(jnp.int32, (1, A_OUT), 1) // HEAD_DIM
    key_col = lax.broadcasted_iota(jnp.int32, (1, 2 * NK), 1)
    no_prev = jnp.where(key_col < NK, NEG, 0.0) * first
    blocks = [(r, j) for r in range(rb) for j in range(tq // NK)]
    blk = lambda ref, r, j: ref[0, r, j * NK:(j + 1) * NK, :]
    scores, values = [], []
    for r, j in blocks:
        qj = blk(q_ref, r, j)
        qs = jnp.concatenate([jnp.where(lane_head == h, qj, jnp.zeros_like(qj)) for h in range(A_HPG)], axis=0)
        k_prev, v_prev = ((kp_ref[0, r], vp_ref[0, r]) if j == 0
                          else (blk(kc_ref, r, j - 1), blk(vc_ref, r, j - 1)))
        s = _dot_nt(qs, jnp.concatenate([k_prev, blk(kc_ref, r, j)], axis=0)) + bias_ref[...]
        scores.append(s + no_prev if j == 0 else s)
        values.append(jnp.concatenate([v_prev, blk(vc_ref, r, j)], axis=0))
    stats = []
    for i in range(len(blocks)):
        m = jnp.max(scores[i], axis=-1, keepdims=True)
        p = jnp.exp(scores[i] - m)
        den = jnp.sum(p, axis=-1, keepdims=True)
        scores[i] = p.astype(BF16)
        stats.append((den, m + jnp.log(den)))
    for i, (r, j) in enumerate(blocks):
        den, lse = stats[i]
        pv = _dot(scores[i], values[i]) / den
        o_acc = jnp.zeros((NK, A_OUT), F32)
        l_acc = jnp.zeros((NK, A_OUT), F32)
        for h in range(A_HPG):
            rows = slice(h * NK, (h + 1) * NK)
            o_acc = jnp.where(lane_head == h, pv[rows], o_acc)
            l_acc = jnp.where(lane_head == h, lse[rows], l_acc)
        o_ref[0, r, j * NK:(j + 1) * NK, :] = o_acc
        lse_ref[0, r, j * NK:(j + 1) * NK, :] = l_acc


def _attn_prompt(qkv, group):
    _, dil = A_GROUPS[group]
    sub = SEQ // dil
    tq = min(ATT_TQ, sub)
    rb = min(dil, ATT_TQ // tq)
    cur = lambda col: pl.BlockSpec((1, rb, tq, A_OUT), lambda b, r, i: (b, r, i, col))
    prev = lambda col: pl.BlockSpec((1, rb, NK, A_OUT),
                                    lambda b, r, i: (b, r, jnp.maximum(i * (tq // NK) - 1, 0), col))
    out = jax.ShapeDtypeStruct((BATCH, dil, sub, A_OUT), F32)
    return pl.pallas_call(
        functools.partial(_attn_p_body, tq=tq, rb=rb),
        grid=(BATCH, dil // rb, sub // tq),
        in_specs=[cur(0), prev(1), cur(1), prev(2), cur(2), _resident((A_HPG * NK, 2 * NK))],
        out_specs=[cur(0), cur(0)],
        out_shape=[out, out],
        compiler_params=_cparams("parallel", "parallel", "arbitrary"),
        name=f"attn_prompt_g{group}",
    )(qkv, qkv, qkv, qkv, qkv, jnp.asarray(_prompt_bias(group)))


SAMP_CACHE_LEN = tuple(win for win, _ in A_GROUPS)
SAMP_NEW_SLOT = 128
SAMP_SEG = []
_off = 0
for _len in SAMP_CACHE_LEN:
    SAMP_SEG.append((_off, _off + _len, _off + _len + SAMP_NEW_SLOT))
    _off += _len + SAMP_NEW_SLOT
SAMP_KEYS = _off
SAMP_Q = A_HPG * DEC_SEQ


def _sample_bias():
    slopes = _alibi_slopes()
    bias = np.full((SAMP_Q, SAMP_KEYS), NEG, np.float64)
    for c in range(SAMP_Q):
        h, t = divmod(c, DEC_SEQ)
        for g, (win, dil) in enumerate(A_GROUPS):
            c0, n0, _ = SAMP_SEG[g]
            sl = slopes[g * A_HPG + h]
            dist = win + t - np.arange(win)
            ok = (dist % dil == 0) & (dist <= win)
            bias[c, c0:c0 + win] = np.where(ok, -sl * dist, NEG)
            for t2 in range(t + 1):
                if (t - t2) % dil == 0:
                    bias[c, n0 + t2] = -sl * (t - t2)
    return bias.astype(np.float32)


def _samp_scores(n, q_ref, k_ref, v_ref, caches, bias_ref):
    lane_head = lax.broadcasted_iota(jnp.int32, (1, A_OUT), 1) // HEAD_DIM
    qn, kn, vn = q_ref[n], k_ref[n], v_ref[n]
    pad = jnp.zeros((SAMP_NEW_SLOT - DEC_SEQ, A_OUT), F32)
    scores, new_v = [], []
    for g in range(len(A_GROUPS)):
        c0, n0, e0 = SAMP_SEG[g]
        cols = slice(g * A_OUT, (g + 1) * A_OUT)
        k_new = jnp.concatenate([kn[:, cols], pad], axis=0).astype(BF16)
        new_v.append(jnp.concatenate([vn[:, cols], pad], axis=0).astype(BF16))
        qbd = jnp.concatenate([jnp.where(lane_head == h, qn[:, cols], 0.0) for h in range(A_HPG)],
                              axis=0).astype(BF16)
        k_t = caches[g][0, n, 0:A_OUT, :].astype(BF16)
        scores.append(_dot(qbd, k_t) + bias_ref[:, c0:n0])
        scores.append(_dot_nt(qbd, k_new) + bias_ref[:, n0:e0])
    return scores, new_v


def _samp_probs(scores):
    m = functools.reduce(jnp.maximum, [jnp.max(s, axis=-1, keepdims=True) for s in scores])
    probs = [jnp.exp(s - m) for s in scores]
    den = functools.reduce(jnp.add, [jnp.sum(p, axis=-1, keepdims=True) for p in probs])
    return [p.astype(BF16) for p in probs], den


def _samp_out(n, probs, den, new_v, caches):
    lane_head = lax.broadcasted_iota(jnp.int32, (1, A_OUT), 1) // HEAD_DIM
    out = jnp.zeros((SAMP_Q, A_OUT), F32)
    for g in range(len(A_GROUPS)):
        v_t = caches[g][0, n, A_OUT:2 * A_OUT, :].astype(BF16)
        out = out + _dot_nt(probs[2 * g], v_t) + _dot(probs[2 * g + 1], new_v[g])
    out = out / den
    attn = jnp.zeros((DEC_SEQ, A_OUT), F32)
    for h in range(A_HPG):
        attn = jnp.where(lane_head == h, out[h * DEC_SEQ:(h + 1) * DEC_SEQ, :], attn)
    return attn


def _proj(h, w_ref, off, width):
    return _dot(h, w_ref[:, off:off + width])


def _gate(h, w_ref, branch):
    return jax.nn.sigmoid(_proj(h, w_ref, OFF_GATES + branch * D_MODEL, D_MODEL))


_MIX_WEIGHT_SHAPES = (
    (1, D_MODEL), (1, D_MODEL),
    (D_MODEL, REST_COLS),
    (A_OUT, D_MODEL), (B_WIDTH, D_MODEL), (C_WIDTH, D_MODEL), (D_WIDTH, D_MODEL), (D_MODEL, D_MODEL),
    (B_CONV, B_WIDTH), (1, C_WIDTH), (1, C_WIDTH),
    (D_CONV, D_WIDTH), (1, D_WIDTH), (1, D_WIDTH), (1, D_WIDTH),
)


def _mixer_p_body(x_ref, o1_ref, o2_ref, o3_ref, l1_ref, l2_ref, l3_ref,
                  pre_ref, post_ref, w_ref, woa_ref, wob_ref, woc_ref, wod_ref, wo_ref,
                  cbw_ref, glng_ref, glnb_ref, cdw_ref, cdb_ref, dlng_ref, dlnb_ref,
                  gws_ref, gbc_ref,
                  xo_ref, nb_ref, nd_ref, zbuf, gbuf, nat, *, tm):
    @pl.when(pl.program_id(1) == 0)
    def _():
        zbuf[0:ZHALO, :] = jnp.zeros((ZHALO, B_WIDTH), F32)
        gbuf[0:GHALO, :] = jnp.zeros((GHALO, D_WIDTH), F32)

    x = x_ref[...]
    h = _rms(x, pre_ref[...]).astype(BF16)

    gbuf[GHALO:GHALO + tm, :] = (_proj(h, w_ref, OFF_GLUA, D_WIDTH)
                                 * jax.nn.sigmoid(_proj(h, w_ref, OFF_GLUG, D_WIDTH)))
    wrows = MIX_RC + GHALO
    dc = []
    for ci in range(tm // MIX_RC):
        win = gbuf[ci * MIX_RC:ci * MIX_RC + wrows, :]
        phase = [win] + [pltpu.roll(win, wrows - s, axis=0) for s in range(1, SUBLANES)]
        acc = jnp.broadcast_to(cdb_ref[...], (MIX_RC, D_WIDTH))
        for k in range(D_CONV):
            a, s = divmod(GHALO - (D_CONV - 1) + k, SUBLANES)
            acc = acc + cdw_ref[k:k + 1, :] * phase[s][a * SUBLANES:a * SUBLANES + MIX_RC, :]
        dc.append(acc)
    tail = gbuf[tm:tm + GHALO, :]
    nd_ref[0] = tail
    gbuf[0:GHALO, :] = tail

    def natural(ref, slot, dil):
        halves = A_OUT // LANES
        for r in range(dil):
            for c in range(halves):
                nat[slot * halves + c, pl.ds(r, tm // dil, stride=dil), :] = ref[0, r, :, c * LANES:(c + 1) * LANES]
        return jnp.concatenate([nat[slot * halves + c] for c in range(halves)], axis=1)

    dil2, dil3 = A_GROUPS[1][1], A_GROUPS[2][1]
    o1, l1 = o1_ref[...], l1_ref[...]
    o2, l2 = natural(o2_ref, 0, dil2), natural(l2_ref, 1, dil2)
    o3, l3 = natural(o3_ref, 2, dil3), natural(l3_ref, 3, dil3)
    mx = jnp.maximum(jnp.maximum(l1, l2), l3)
    e1, e2, e3 = jnp.exp(l1 - mx), jnp.exp(l2 - mx), jnp.exp(l3 - mx)
    attn = (e1 * o1 + e2 * o2 + e3 * o3) / (e1 + e2 + e3)


    zbuf[ZHALO:ZHALO + tm, :] = _proj(h, w_ref, OFF_CGATE, B_WIDTH) * _proj(h, w_ref, OFF_BIN, B_WIDTH)
    b_gate = _proj(h, w_ref, OFF_BGATE, B_WIDTH)
    merged = _gate(h, w_ref, 0) * _dot(attn.astype(BF16), woa_ref[...])
    conv = jnp.zeros((tm, B_WIDTH), F32)
    for k in range(B_CONV):
        start = ZHALO - (B_CONV - 1) + k
        conv = conv + cbw_ref[k:k + 1, :] * zbuf[start:start + tm, :]
    tail = zbuf[tm:tm + ZHALO, :]
    nb_ref[0] = tail
    zbuf[0:ZHALO, :] = tail

    vn = _ln(_proj(h, w_ref, OFF_GV, C_WIDTH), glng_ref[...], glnb_ref[...]).astype(BF16)
    u = _proj(h, w_ref, OFF_U, C_WIDTH)
    gate1 = _gate(h, w_ref, 1)
    merged = merged + gate1 * _dot((b_gate * conv).astype(BF16), wob_ref[...])
    gate2 = _gate(h, w_ref, 2)
    r_i = lax.broadcasted_iota(jnp.int32, (C_CHUNK, C_CHUNK), 0)
    c_i = lax.broadcasted_iota(jnp.int32, (C_CHUNK, C_CHUNK), 1)
    ws = [jnp.where(r_i >= c_i, gws_ref[g], 0.0).astype(BF16) for g in range(C_GROUPS)]
    chunks = []
    for c in range(tm // C_CHUNK):
        rows = slice(c * C_CHUNK, (c + 1) * C_CHUNK)
        chunks.append(jnp.concatenate(
            [_dot(ws[g], vn[rows, g * C_GDIM:(g + 1) * C_GDIM]) + gbc_ref[:, g:g + 1]
             for g in range(C_GROUPS)], axis=1))
    mixed = jnp.concatenate(chunks, axis=0)
    gate3 = _gate(h, w_ref, 3)
    merged = merged + gate2 * _dot((u * mixed).astype(BF16), woc_ref[...])

    dn = _ln(jnp.concatenate(dc, axis=0), dlng_ref[...], dlnb_ref[...])
    merged = merged + gate3 * _dot(jax.nn.silu(dn).astype(BF16), wod_ref[...])

    y = _dot(merged.astype(BF16), wo_ref[...])
    xo_ref[...] = x + _rms(y, post_ref[...])


def _mixer_prompt(x, o, lse, weights, gws, gb_col, layer):
    tm = MIX_TM
    steps = SEQ // tm
    row = lambda width: pl.BlockSpec((tm, width), lambda b, j: (b * steps + j, 0))
    tail = lambda n: pl.BlockSpec((1, n, B_WIDTH), lambda b, j: (b, 0, 0))
    res = lambda dil: pl.BlockSpec((1, dil, tm // dil, A_OUT), lambda b, j: (b, 0, j, 0))
    attn_specs = [row(A_OUT)] + [res(dil) for _, dil in A_GROUPS[1:]]
    flat = lambda a: a.reshape(BATCH * SEQ, A_OUT)
    return pl.pallas_call(
        functools.partial(_mixer_p_body, tm=tm),
        grid=(BATCH, steps),
        in_specs=([row(D_MODEL)] + attn_specs + attn_specs
                  + [_of_layer(s, layer) for s in _MIX_WEIGHT_SHAPES]
                  + [_of_layer((C_GROUPS, C_CHUNK, C_CHUNK), layer), _of_layer((C_CHUNK, C_GROUPS), layer)]),
        out_specs=[row(D_MODEL), tail(ZHALO), tail(GHALO)],
        out_shape=[jax.ShapeDtypeStruct((BATCH * SEQ, D_MODEL), F32),
                   jax.ShapeDtypeStruct((BATCH, ZHALO, B_WIDTH), F32),
                   jax.ShapeDtypeStruct((BATCH, GHALO, D_WIDTH), F32)],
        scratch_shapes=[pltpu.VMEM((tm + ZHALO, B_WIDTH), F32),
                        pltpu.VMEM((tm + GHALO, D_WIDTH), F32),
                        pltpu.VMEM((4 * A_OUT // LANES, tm, LANES), F32)],
        compiler_params=_cparams("parallel", "arbitrary"),
        name="mixer_prompt",
    )(x, flat(o[0]), o[1], o[2], flat(lse[0]), lse[1], lse[2], *weights, gws, gb_col)


def _mixer_s_body(x_ref, attn_ref, sb_ref, sd_ref,
                  pre_ref, post_ref, w_ref, woa_ref, wob_ref, woc_ref, wod_ref, wo_ref,
                  cbw_ref, glng_ref, glnb_ref, cdw_ref, cdb_ref, dlng_ref, dlnb_ref,
                  wtab_ref, btab_ref,
                  xo_ref, nb_ref, vn_ref, glu_ref, *, sn):
    rows = DEC_SEQ * sn
    x = x_ref[...].reshape(rows, D_MODEL)
    h = _rms(x, pre_ref[...]).astype(BF16)
    slab = lambda a, t: a[t * sn:(t + 1) * sn, :]

    merged = _gate(h, w_ref, 0) * _dot(attn_ref[...].reshape(rows, A_OUT).astype(BF16), woa_ref[...])

    zc = _proj(h, w_ref, OFF_CGATE, B_WIDTH) * _proj(h, w_ref, OFF_BIN, B_WIDTH)
    zpad = [sb_ref[i] for i in range(B_CONV - 1)] + [slab(zc, t) for t in range(DEC_SEQ)]
    conv = jnp.concatenate(
        [sum(cbw_ref[k:k + 1, :] * zpad[t + k] for k in range(B_CONV)) for t in range(DEC_SEQ)], axis=0)
    br = _dot((_proj(h, w_ref, OFF_BGATE, B_WIDTH) * conv).astype(BF16), wob_ref[...])
    merged = merged + _gate(h, w_ref, 1) * br
    for i in range(B_CONV - 1):
        nb_ref[i] = zpad[DEC_SEQ + i]

    vn = _ln(_proj(h, w_ref, OFF_GV, C_WIDTH), glng_ref[...], glnb_ref[...])
    vn_ref[...] = vn.reshape(DEC_SEQ, sn, C_WIDTH)
    mixed = []
    for t in range(DEC_SEQ):
        acc = jnp.broadcast_to(btab_ref[t:t + 1, :], (sn, C_WIDTH))
        for s in range(t + 1):
            acc = acc + wtab_ref[t * DEC_SEQ + s:t * DEC_SEQ + s + 1, :] * slab(vn, s)
        mixed.append(acc)
    mixed = jnp.concatenate(mixed, axis=0)
    br = _dot((_proj(h, w_ref, OFF_U, C_WIDTH) * mixed).astype(BF16), woc_ref[...])
    merged = merged + _gate(h, w_ref, 2) * br

    glu = _proj(h, w_ref, OFF_GLUA, D_WIDTH) * jax.nn.sigmoid(_proj(h, w_ref, OFF_GLUG, D_WIDTH))
    glu_ref[...] = glu.reshape(DEC_SEQ, sn, D_WIDTH)
    gpad = [sd_ref[i] for i in range(D_CONV - 1)] + [slab(glu, t) for t in range(DEC_SEQ)]
    dc = []
    for t in range(DEC_SEQ):
        acc = jnp.broadcast_to(cdb_ref[...], (sn, D_WIDTH))
        for k in range(D_CONV):
            acc = acc + cdw_ref[k:k + 1, :] * gpad[t + k]
        dc.append(acc)
    dn = _ln(jnp.concatenate(dc, axis=0), dlng_ref[...], dlnb_ref[...])
    br = _dot(jax.nn.silu(dn).astype(BF16), wod_ref[...])
    merged = merged + _gate(h, w_ref, 3) * br

    y = _dot(merged.astype(BF16), wo_ref[...])
    xo_ref[...] = (x + _rms(y, post_ref[...])).reshape(DEC_SEQ, sn, D_MODEL)


def _mixer_sample(x, attn, state_b, state_d, weights, wtab, btab, layer):
    sn = SAMP_SN
    tslab = lambda n, width: pl.BlockSpec((n, sn, width), lambda i: (0, i, 0))
    return pl.pallas_call(
        functools.partial(_mixer_s_body, sn=sn),
        grid=(DEC_BATCH // sn,),
        in_specs=([tslab(DEC_SEQ, D_MODEL), tslab(DEC_SEQ, A_OUT),
                   tslab(B_CONV - 1, B_WIDTH), tslab(D_CONV - 1, D_WIDTH)]
                  + [_of_layer(s, layer) for s in _MIX_WEIGHT_SHAPES]
                  + [_resident((DEC_SEQ * DEC_SEQ, C_WIDTH)), _resident((DEC_SEQ, C_WIDTH))]),
        out_specs=[tslab(DEC_SEQ, D_MODEL), tslab(B_CONV - 1, B_WIDTH),
                   tslab(DEC_SEQ, C_WIDTH), tslab(DEC_SEQ, D_WIDTH)],
        out_shape=[jax.ShapeDtypeStruct((DEC_SEQ, DEC_BATCH, D_MODEL), F32),
                   jax.ShapeDtypeStruct((B_CONV - 1, DEC_BATCH, B_WIDTH), F32),
                   jax.ShapeDtypeStruct((DEC_SEQ, DEC_BATCH, C_WIDTH), F32),
                   jax.ShapeDtypeStruct((DEC_SEQ, DEC_BATCH, D_WIDTH), F32)],
        compiler_params=_cparams("parallel"),
        name="mixer_sample",
    )(x, attn, state_b, state_d, *weights, wtab, btab)


def kernel(x_prompt, x_sample, cache_attn_w128, cache_attn_w512, cache_attn_w2048, state_conv_b, state_conv_d, ffn1_pre_g, ffn1_post_g, ffn1_w_gate, ffn1_w_up, ffn1_w_down, mix_pre_g, mix_post_g, w_in, w_out_a, conv_b_w, w_out_b, gmlp_ln_g, gmlp_ln_b, gmlp_ws, gmlp_b, w_out_c, conv_d_w, conv_d_b, conv_d_ln_g, conv_d_ln_b, w_out_d, w_o, ffn2_pre_g, ffn2_post_g, ffn2_w_gate, ffn2_w_up, ffn2_w_down):
    bf = lambda a: a.astype(BF16)
    vec = lambda a: a.reshape(DEPTH, 1, -1)

    xp = x_prompt.reshape(BATCH * SEQ, D_MODEL)
    xs = x_sample.transpose(1, 0, 2).reshape(DEC_SEQ * DEC_BATCH, D_MODEL)
    time_minor = lambda c: c.transpose(0, 1, 3, 4, 5, 2).reshape(DEPTH, DEC_BATCH, 2 * A_OUT, c.shape[2])
    c1, c2, c3 = time_minor(cache_attn_w128), time_minor(cache_attn_w512), time_minor(cache_attn_w2048)
    state_b_tm = state_conv_b.transpose(0, 2, 1, 3)
    state_d_tm = state_conv_d.transpose(0, 2, 1, 3)

    ffn1 = (vec(ffn1_pre_g), vec(ffn1_post_g), bf(ffn1_w_gate), bf(ffn1_w_up), bf(ffn1_w_down))
    ffn2 = (vec(ffn2_pre_g), vec(ffn2_post_g), bf(ffn2_w_gate), bf(ffn2_w_up), bf(ffn2_w_down))
    w_qkv, w_rest = bf(w_in[:, :, :QKV_COLS]), bf(w_in[:, :, QKV_COLS:])
    pre = vec(mix_pre_g)
    mix_w = (pre, vec(mix_post_g), w_rest,
             bf(w_out_a), bf(w_out_b), bf(w_out_c), bf(w_out_d), bf(w_o),
             conv_b_w, vec(gmlp_ln_g), vec(gmlp_ln_b),
             conv_d_w, vec(conv_d_b), vec(conv_d_ln_g), vec(conv_d_ln_b))
    gb_col = gmlp_b.transpose(0, 2, 1)

    pa, sa = [[] for _ in A_GROUPS], [[] for _ in A_GROUPS]
    pb, sb, sc, pd, sd = [], [], [], [], []
    for l in range(DEPTH):
        xs = _ffn(xs, *ffn1, l)
        qs, ks, vs = _qkv_sample(xs, pre, w_qkv, l)
        seq_major = lambda a: a.astype(F32).reshape(DEC_SEQ, DEC_BATCH, A_WIDTH).transpose(1, 0, 2)
        qs, ks, vs = seq_major(qs), seq_major(ks), seq_major(vs)
        half = DEC_BATCH // 2

        xp, attn_lo = _ffn(xp, *ffn1, l, attn=(qs, ks, vs, c1, c2, c3, 0))
        *qkv, kvt1, kvt2, kvt3 = _qkv_prompt(xp, pre, w_qkv, l)
        outs = [_attn_prompt(qkv[g], g) for g in range(len(A_GROUPS))]
        xp, nb, nd = _mixer_prompt(xp, [o for o, _ in outs], [s for _, s in outs], mix_w, gmlp_ws, gb_col, l)
        xp, attn_hi = _ffn(xp, *ffn2, l, attn=(qs, ks, vs, c1, c2, c3, half))
        for g, kvt in enumerate((kvt1, kvt2, kvt3)):
            pa[g].append(kvt)
        pb.append(nb[:, ZHALO - (B_CONV - 1):])
        pd.append(nd[:, GHALO - (D_CONV - 1):])

        attn_s = jnp.concatenate([attn_lo, attn_hi], axis=0).transpose(1, 0, 2)
        ws8 = gmlp_ws[l][:, :DEC_SEQ, :DEC_SEQ]
        wtab = jnp.repeat(ws8.transpose(1, 2, 0).reshape(DEC_SEQ * DEC_SEQ, C_GROUPS), C_GDIM, axis=1)
        btab = jnp.repeat(gmlp_b[l][:, :DEC_SEQ].T, C_GDIM, axis=1)
        xs3, nb_s, vn_s, glu_s = _mixer_sample(
            xs.reshape(DEC_SEQ, DEC_BATCH, D_MODEL), attn_s, state_b_tm[l], state_d_tm[l], mix_w, wtab, btab, l)
        xs = _ffn(xs3.reshape(DEC_SEQ * DEC_BATCH, D_MODEL), *ffn2, l)
        k4 = ks.reshape(DEC_BATCH, DEC_SEQ, A_HEADS, HEAD_DIM)
        v4 = vs.reshape(DEC_BATCH, DEC_SEQ, A_HEADS, HEAD_DIM)
        for g in range(len(A_GROUPS)):
            hs = slice(g * A_HPG, (g + 1) * A_HPG)
            sa[g].append(jnp.stack([k4[:, :, hs], v4[:, :, hs]], axis=2))
        sb.append(nb_s.transpose(1, 0, 2))
        sc.append(vn_s.transpose(1, 0, 2))
        sd.append(jnp.concatenate([state_conv_d[l][:, DEC_SEQ:], glu_s.transpose(1, 0, 2)], axis=1))

    st = lambda xs_: jnp.stack(xs_, axis=0)
    new_kv = lambda ts: st(ts).reshape(DEPTH, BATCH, 2, A_HPG, HEAD_DIM, -1).transpose(0, 1, 5, 2, 3, 4)
    return (xp.reshape(BATCH, SEQ, D_MODEL),
            xs.reshape(DEC_SEQ, DEC_BATCH, D_MODEL).transpose(1, 0, 2),
            new_kv(pa[0]), new_kv(pa[1]), new_kv(pa[2]), st(sa[0]), st(sa[1]), st(sa[2]),
            st(pb), st(sb), st(sc), st(pd), st(sd))
```

```python
import functools

import numpy as np
import jax
import jax.numpy as jnp
from jax import lax
from jax.experimental import pallas as pl
from jax.experimental.pallas import tpu as pltpu

D_MODEL = 1024
BATCH = 4
SEQ = 4096
DEPTH = 2
DEC_BATCH = 128
DEC_SEQ = 8
HEAD_DIM = 64
A_GROUPS = ((128, 1), (512, 4), (2048, 16))
A_HPG = 4
A_HEADS = A_HPG * len(A_GROUPS)
A_WIDTH = A_HEADS * HEAD_DIM
A_OUT = A_HPG * HEAD_DIM
NK = 128
B_WIDTH = 512
B_CONV = 3
C_WIDTH = 512
C_CHUNK = 128
C_GROUPS = 4
C_GDIM = 128
D_WIDTH = 512
D_CONV = 31
D_FF = 2816
EPS = 1e-6
QKV_COLS = 3 * A_WIDTH
OFF_BGATE, OFF_CGATE, OFF_BIN, OFF_U, OFF_GV, OFF_GLUA, OFF_GLUG, OFF_GATES = (
    0, 512, 1024, 1536, 2048, 2560, 3072, 3584)
REST_COLS = OFF_GATES + 4 * D_MODEL

F32 = jnp.float32
BF16 = jnp.bfloat16
NEG = -1e30
SUBLANES = 8
LANES = 128
V7X_VMEM_LIMIT_BYTES = 56 * 1024 * 1024

FFN_TM = 512
FFN_FC = 256
QKV_TM = 512
MIX_TM = 512
MIX_RC = 64
ZHALO = 8
GHALO = 32
ATT_TQ = 1024
SAMP_SB = 2
SAMP_SN = 32


def _cparams(*sem):
    return pltpu.CompilerParams(dimension_semantics=sem, vmem_limit_bytes=V7X_VMEM_LIMIT_BYTES)


def _resident(shape):
    nd = len(shape)
    return pl.BlockSpec(shape, lambda *_: (0,) * nd, pipeline_mode=pl.Buffered(1))


def _of_layer(shape, layer):
    nd = len(shape)
    return pl.BlockSpec((None,) + tuple(shape), lambda *_: (layer,) + (0,) * nd, pipeline_mode=pl.Buffered(1))


def _dot(a, b):
    return jnp.dot(a, b, preferred_element_type=F32)


def _dot_nt(a, b):
    return lax.dot_general(a, b, (((1,), (1,)), ((), ())), preferred_element_type=F32)


def _rms(x, g):
    return x * lax.rsqrt(jnp.mean(x * x, axis=-1, keepdims=True) + EPS) * g


def _ln(x, g, b):
    mu = jnp.mean(x, axis=-1, keepdims=True)
    xc = x - mu
    var = jnp.mean(xc * xc, axis=-1, keepdims=True)
    return xc * lax.rsqrt(var + EPS) * g + b


def _ffn_body(x_ref, pre_ref, post_ref, wg_ref, wu_ref, wd_ref, *rest, sb):
    if sb:
        q_ref, k_ref, v_ref, c1_ref, c2_ref, c3_ref, bias_ref, o_ref, attn_ref = rest
        caches = (c1_ref, c2_ref, c3_ref)
        staged = [_samp_scores(n, q_ref, k_ref, v_ref, caches, bias_ref) for n in range(sb)]
        staged = [_samp_probs(scores) + (new_v,) for scores, new_v in staged]
    else:
        (o_ref,) = rest
    x = x_ref[...]
    h = _rms(x, pre_ref[...]).astype(BF16)
    acc = jnp.zeros(x.shape, F32)
    chunks = D_FF // FFN_FC
    for c in range(chunks):
        if sb and c == chunks // 2:
            for n, (probs, den, new_v) in enumerate(staged):
                attn_ref[n] = _samp_out(n, probs, den, new_v, caches)
        sl = slice(c * FFN_FC, (c + 1) * FFN_FC)
        g = _dot(h, wg_ref[:, sl])
        u = _dot(h, wu_ref[:, sl])
        acc = acc + _dot((jax.nn.silu(g) * u).astype(BF16), wd_ref[sl, :])
    o_ref[...] = x + 0.5 * _rms(acc, post_ref[...])


def _ffn(x, pre_g, post_g, wg, wu, wd, layer, attn=None):
    rows = x.shape[0]
    tm = min(FFN_TM, rows)
    steps = rows // tm
    in_specs = [pl.BlockSpec((tm, D_MODEL), lambda i: (i, 0)),
                _of_layer((1, D_MODEL), layer), _of_layer((1, D_MODEL), layer),
                _of_layer((D_MODEL, D_FF), layer), _of_layer((D_MODEL, D_FF), layer),
                _of_layer((D_FF, D_MODEL), layer)]
    out_specs = [pl.BlockSpec((tm, D_MODEL), lambda i: (i, 0))]
    out_shape = [jax.ShapeDtypeStruct((rows, D_MODEL), F32)]
    operands = [x, pre_g, post_g, wg, wu, wd]
    sb = 0
    if attn is not None:
        *operands_a, first_seq = attn
        sb = SAMP_SB
        blk0 = first_seq // sb
        qspec = pl.BlockSpec((sb, DEC_SEQ, A_WIDTH), lambda i: (i + blk0, 0, 0))
        cspec = lambda n: pl.BlockSpec((1, sb, 2 * A_OUT, n), lambda i: (layer, i + blk0, 0, 0))
        in_specs += [qspec, qspec, qspec] + [cspec(n) for n in SAMP_CACHE_LEN] + [_resident((SAMP_Q, SAMP_KEYS))]
        out_specs.append(pl.BlockSpec((sb, DEC_SEQ, A_OUT), lambda i: (i, 0, 0)))
        out_shape.append(jax.ShapeDtypeStruct((steps * sb, DEC_SEQ, A_OUT), F32))
        operands += operands_a + [jnp.asarray(_sample_bias())]
    out = pl.pallas_call(
        functools.partial(_ffn_body, sb=sb),
        grid=(steps,),
        in_specs=in_specs,
        out_specs=out_specs,
        out_shape=out_shape,
        compiler_params=_cparams("parallel"),
        name="ffn_attn" if sb else "ffn",
    )(*operands)
    return out if sb else out[0]


Q_SCALE = HEAD_DIM ** -0.5


def _qkv_s_body(x_ref, g_ref, w_ref, q_ref, k_ref, v_ref):
    h = _rms(x_ref[...], g_ref[...]).astype(BF16)
    q_ref[...] = (_dot(h, w_ref[:, 0:A_WIDTH]) * Q_SCALE).astype(BF16)
    k_ref[...] = _dot(h, w_ref[:, A_WIDTH:2 * A_WIDTH]).astype(BF16)
    v_ref[...] = _dot(h, w_ref[:, 2 * A_WIDTH:3 * A_WIDTH]).astype(BF16)


def _qkv_sample(x, g, w_qkv, layer):
    rows = x.shape[0]
    tm = min(QKV_TM, rows)
    out = jax.ShapeDtypeStruct((rows, A_WIDTH), BF16)
    spec = pl.BlockSpec((tm, A_WIDTH), lambda i: (i, 0))
    return pl.pallas_call(
        _qkv_s_body,
        grid=(rows // tm,),
        in_specs=[pl.BlockSpec((tm, D_MODEL), lambda i: (i, 0)),
                  _of_layer((1, D_MODEL), layer), _of_layer((D_MODEL, QKV_COLS), layer)],
        out_specs=[spec, spec, spec],
        out_shape=[out, out, out],
        compiler_params=_cparams("parallel"),
        name="qkv_sample",
    )(x, g, w_qkv)


def _kv_tail(group, tm):
    keep = min(A_GROUPS[group][0], SEQ)
    rows = min(keep, tm)
    return (SEQ - keep) // tm, rows, tm - rows


def _qkv_p_body(x_ref, g_ref, w_ref, o1_ref, o2_ref, o3_ref, t1_ref, t2_ref, t3_ref, acc_ref, *, tm):
    h = _rms(x_ref[...], g_ref[...]).astype(BF16)
    for g, (_, dil) in enumerate(A_GROUPS):
        for part in range(3):
            src = part * A_WIDTH + g * A_OUT
            res = _dot(h, w_ref[:, src:src + A_OUT])
            for sub in range(A_OUT // LANES):
                acc_ref[src // LANES + sub] = res[:, sub * LANES:(sub + 1) * LANES]
        out = (o1_ref, o2_ref, o3_ref)[g]
        for c in range(3 * A_OUT // LANES):
            part, sub = divmod(c * LANES, A_OUT)
            src = (part * A_WIDTH + g * A_OUT + sub) // LANES
            for r in range(dil):
                val = acc_ref[src] if dil == 1 else acc_ref[src, pl.ds(r, tm // dil, stride=dil), :]
                if part == 0:
                    val = val * Q_SCALE
                out[0, r, :, c * LANES:(c + 1) * LANES] = val.astype(BF16)
    for g in range(len(A_GROUPS)):
        first, rows, row0 = _kv_tail(g, tm)
        out_t = (t1_ref, t2_ref, t3_ref)[g]

        @pl.when(pl.program_id(1) >= first)
        def _(g=g, rows=rows, row0=row0, out_t=out_t):
            for c in range(2 * A_OUT // LANES):
                part, sub = divmod(c * LANES, A_OUT)
                src = ((part + 1) * A_WIDTH + g * A_OUT + sub) // LANES
                out_t[0, c * LANES:(c + 1) * LANES, :] = acc_ref[src, row0:row0 + rows, :].T


def _qkv_prompt(x, g, w_qkv, layer):
    tm = QKV_TM
    steps = SEQ // tm
    dils = [dil for _, dil in A_GROUPS]
    tails = [_kv_tail(g, tm) for g in range(len(A_GROUPS))]
    tail_spec = lambda first, rows: pl.BlockSpec((1, 2 * A_OUT, rows), lambda b, j: (b, 0, jnp.maximum(j - first, 0)))
    return pl.pallas_call(
        functools.partial(_qkv_p_body, tm=tm),
        grid=(BATCH, steps),
        in_specs=[pl.BlockSpec((tm, D_MODEL), lambda b, j: (b * steps + j, 0)),
                  _of_layer((1, D_MODEL), layer), _of_layer((D_MODEL, QKV_COLS), layer)],
        out_specs=([pl.BlockSpec((1, dil, tm // dil, 3 * A_OUT), lambda b, j: (b, 0, j, 0)) for dil in dils]
                   + [tail_spec(first, rows) for first, rows, _ in tails]),
        out_shape=([jax.ShapeDtypeStruct((BATCH, dil, SEQ // dil, 3 * A_OUT), BF16) for dil in dils]
                   + [jax.ShapeDtypeStruct((BATCH, 2 * A_OUT, min(win, SEQ)), F32) for win, _ in A_GROUPS]),
        scratch_shapes=[pltpu.VMEM((QKV_COLS // LANES, tm, LANES), F32)],
        compiler_params=_cparams("parallel", "arbitrary"),
        name="qkv_prompt",
    )(x, g, w_qkv)


def _alibi_slopes():
    return np.exp2(-8.0 * np.arange(1, A_HEADS + 1, dtype=np.float64) / A_HEADS)


def _prompt_bias(group):
    _, dil = A_GROUPS[group]
    rel = NK + np.arange(NK)[:, None] - np.arange(2 * NK)[None, :]
    valid = (rel >= 0) & (rel <= NK)
    slopes = _alibi_slopes()[group * A_HPG:(group + 1) * A_HPG]
    bias = -slopes[:, None, None] * (dil * rel)[None].astype(np.float64)
    return np.where(valid[None], bias, NEG).astype(np.float32).reshape(A_HPG * NK, 2 * NK)


def _attn_p_body(q_ref, kp_ref, kc_ref, vp_ref, vc_ref, bias_ref, o_ref, lse_ref, *, tq, rb):
    first = (pl.program_id(2) == 0).astype(F32)
    lane_head = lax.broadcasted_iota(jnp.int32, (1, A_OUT), 1) // HEAD_DIM
    key_col = lax.broadcasted_iota(jnp.int32, (1, 2 * NK), 1)
    no_prev = jnp.where(key_col < NK, NEG, 0.0) * first
    blocks = [(r, j) for r in range(rb) for j in range(tq // NK)]
    blk = lambda ref, r, j: ref[0, r, j * NK:(j + 1) * NK, :]
    scores, values = [], []
    for r, j in blocks:
        qj = blk(q_ref, r, j)
        qs = jnp.concatenate([jnp.where(lane_head == h, qj, jnp.zeros_like(qj)) for h in range(A_HPG)], axis=0)
        k_prev, v_prev = ((kp_ref[0, r], vp_ref[0, r]) if j == 0
                          else (blk(kc_ref, r, j - 1), blk(vc_ref, r, j - 1)))
        s = _dot_nt(qs, jnp.concatenate([k_prev, blk(kc_ref, r, j)], axis=0)) + bias_ref[...]
        scores.append(s + no_prev if j == 0 else s)
        values.append(jnp.concatenate([v_prev, blk(vc_ref, r, j)], axis=0))
    stats = []
    for i in range(len(blocks)):
        m = jnp.max(scores[i], axis=-1, keepdims=True)
        p = jnp.exp(scores[i] - m)
        den = jnp.sum(p, axis=-1, keepdims=True)
        scores[i] = p.astype(BF16)
        stats.append((den, m + jnp.log(den)))
    for i, (r, j) in enumerate(blocks):
        den, lse = stats[i]
        pv = _dot(scores[i], values[i]) / den
        o_acc = jnp.zeros((NK, A_OUT), F32)
        l_acc = jnp.zeros((NK, A_OUT), F32)
        for h in range(A_HPG):
            rows = slice(h * NK, (h + 1) * NK)
            o_acc = jnp.where(lane_head == h, pv[rows], o_acc)
            l_acc = jnp.where(lane_head == h, lse[rows], l_acc)
        o_ref[0, r, j * NK:(j + 1) * NK, :] = o_acc
        lse_ref[0, r, j * NK:(j + 1) * NK, :] = l_acc


def _attn_prompt(qkv, group):
    _, dil = A_GROUPS[group]
    sub = SEQ // dil
    tq = min(ATT_TQ, sub)
    rb = min(dil, ATT_TQ // tq)
    cur = lambda col: pl.BlockSpec((1, rb, tq, A_OUT), lambda b, r, i: (b, r, i, col))
    prev = lambda col: pl.BlockSpec((1, rb, NK, A_OUT),
                                    lambda b, r, i: (b, r, jnp.maximum(i * (tq // NK) - 1, 0), col))
    out = jax.ShapeDtypeStruct((BATCH, dil, sub, A_OUT), F32)
    return pl.pallas_call(
        functools.partial(_attn_p_body, tq=tq, rb=rb),
        grid=(BATCH, dil // rb, sub // tq),
        in_specs=[cur(0), prev(1), cur(1), prev(2), cur(2), _resident((A_HPG * NK, 2 * NK))],
        out_specs=[cur(0), cur(0)],
        out_shape=[out, out],
        compiler_params=_cparams("parallel", "parallel", "arbitrary"),
        name=f"attn_prompt_g{group}",
    )(qkv, qkv, qkv, qkv, qkv, jnp.asarray(_prompt_bias(group)))


SAMP_CACHE_LEN = tuple(win for win, _ in A_GROUPS)
SAMP_NEW_SLOT = 128
SAMP_SEG = []
_off = 0
for _len in SAMP_CACHE_LEN:
    SAMP_SEG.append((_off, _off + _len, _off + _len + SAMP_NEW_SLOT))
    _off += _len + SAMP_NEW_SLOT
SAMP_KEYS = _off
SAMP_Q = A_HPG * DEC_SEQ


def _sample_bias():
    slopes = _alibi_slopes()
    bias = np.full((SAMP_Q, SAMP_KEYS), NEG, np.float64)
    for c in range(SAMP_Q):
        h, t = divmod(c, DEC_SEQ)
        for g, (win, dil) in enumerate(A_GROUPS):
            c0, n0, _ = SAMP_SEG[g]
            sl = slopes[g * A_HPG + h]
            dist = win + t - np.arange(win)
            ok = (dist % dil == 0) & (dist <= win)
            bias[c, c0:c0 + win] = np.where(ok, -sl * dist, NEG)
            for t2 in range(t + 1):
                if (t - t2) % dil == 0:
                    bias[c, n0 + t2] = -sl * (t - t2)
    return bias.astype(np.float32)


def _samp_scores(n, q_ref, k_ref, v_ref, caches, bias_ref):
    lane_head = lax.broadcasted_iota(jnp.int32, (1, A_OUT), 1) // HEAD_DIM
    qn, kn, vn = q_ref[n], k_ref[n], v_ref[n]
    pad = jnp.zeros((SAMP_NEW_SLOT - DEC_SEQ, A_OUT), F32)
    scores, new_v = [], []
    for g in range(len(A_GROUPS)):
        c0, n0, e0 = SAMP_SEG[g]
        cols = slice(g * A_OUT, (g + 1) * A_OUT)
        k_new = jnp.concatenate([kn[:, cols], pad], axis=0).astype(BF16)
        new_v.append(jnp.concatenate([vn[:, cols], pad], axis=0).astype(BF16))
        qbd = jnp.concatenate([jnp.where(lane_head == h, qn[:, cols], 0.0) for h in range(A_HPG)],
                              axis=0).astype(BF16)
        k_t = caches[g][0, n, 0:A_OUT, :].astype(BF16)
        scores.append(_dot(qbd, k_t) + bias_ref[:, c0:n0])
        scores.append(_dot_nt(qbd, k_new) + bias_ref[:, n0:e0])
    return scores, new_v


def _samp_probs(scores):
    m = functools.reduce(jnp.maximum, [jnp.max(s, axis=-1, keepdims=True) for s in scores])
    probs = [jnp.exp(s - m) for s in scores]
    den = functools.reduce(jnp.add, [jnp.sum(p, axis=-1, keepdims=True) for p in probs])
    return [p.astype(BF16) for p in probs], den


def _samp_out(n, probs, den, new_v, caches):
    lane_head = lax.broadcasted_iota(jnp.int32, (1, A_OUT), 1) // HEAD_DIM
    out = jnp.zeros((SAMP_Q, A_OUT), F32)
    for g in range(len(A_GROUPS)):
        v_t = caches[g][0, n, A_OUT:2 * A_OUT, :].astype(BF16)
        out = out + _dot_nt(probs[2 * g], v_t) + _dot(probs[2 * g + 1], new_v[g])
    out = out / den
    attn = jnp.zeros((DEC_SEQ, A_OUT), F32)
    for h in range(A_HPG):
        attn = jnp.where(lane_head == h, out[h * DEC_SEQ:(h + 1) * DEC_SEQ, :], attn)
    return attn


def _proj(h, w_ref, off, width):
    return _dot(h, w_ref[0, :, off:off + width])


def _gate(h, w_ref, branch):
    return jax.nn.sigmoid(_proj(h, w_ref, OFF_GATES + branch * D_MODEL, D_MODEL))


_MIX_WEIGHT_SHAPES = (
    (1, D_MODEL), (1, D_MODEL),
    (D_MODEL, REST_COLS),
    (A_OUT, D_MODEL), (B_WIDTH, D_MODEL), (C_WIDTH, D_MODEL), (D_WIDTH, D_MODEL), (D_MODEL, D_MODEL),
    (B_CONV, B_WIDTH), (1, C_WIDTH), (1, C_WIDTH),
    (D_CONV, D_WIDTH), (1, D_WIDTH), (1, D_WIDTH), (1, D_WIDTH),
)
_MIX_W_IN = 2


def _mix_weight_specs(layer):
    specs = [_of_layer(s, layer) for s in _MIX_WEIGHT_SHAPES]
    specs[_MIX_W_IN] = pl.BlockSpec((pl.Element(1), pl.Element(D_MODEL), pl.Element(REST_COLS)),
                                    lambda *_: (layer, 0, QKV_COLS), pipeline_mode=pl.Buffered(1))
    return specs


def _mixer_p_body(x_ref, o1_ref, o2_ref, o3_ref, l1_ref, l2_ref, l3_ref,
                  pre_ref, post_ref, w_ref, woa_ref, wob_ref, woc_ref, wod_ref, wo_ref,
                  cbw_ref, glng_ref, glnb_ref, cdw_ref, cdb_ref, dlng_ref, dlnb_ref,
                  gws_ref, gbc_ref,
                  xo_ref, nb_ref, nd_ref, zbuf, gbuf, nat, *, tm):
    @pl.when(pl.program_id(1) == 0)
    def _():
        zbuf[0:ZHALO, :] = jnp.zeros((ZHALO, B_WIDTH), F32)
        gbuf[0:GHALO, :] = jnp.zeros((GHALO, D_WIDTH), F32)

    x = x_ref[...]
    h = _rms(x, pre_ref[...]).astype(BF16)

    gbuf[GHALO:GHALO + tm, :] = (_proj(h, w_ref, OFF_GLUA, D_WIDTH)
                                 * jax.nn.sigmoid(_proj(h, w_ref, OFF_GLUG, D_WIDTH)))
    wrows = MIX_RC + GHALO
    dc = []
    for ci in range(tm // MIX_RC):
        win = gbuf[ci * MIX_RC:ci * MIX_RC + wrows, :]
        phase = [win] + [pltpu.roll(win, wrows - s, axis=0) for s in range(1, SUBLANES)]
        acc = jnp.broadcast_to(cdb_ref[...], (MIX_RC, D_WIDTH))
        for k in range(D_CONV):
            a, s = divmod(GHALO - (D_CONV - 1) + k, SUBLANES)
            acc = acc + cdw_ref[k:k + 1, :] * phase[s][a * SUBLANES:a * SUBLANES + MIX_RC, :]
        dc.append(acc)
    tail = gbuf[tm:tm + GHALO, :]
    nd_ref[0] = tail
    gbuf[0:GHALO, :] = tail

    def natural(ref, slot, dil):
        halves = A_OUT // LANES
        for r in range(dil):
            for c in range(halves):
                nat[slot * halves + c, pl.ds(r, tm // dil, stride=dil), :] = ref[0, r, :, c * LANES:(c + 1) * LANES]
        return jnp.concatenate([nat[slot * halves + c] for c in range(halves)], axis=1)

    dil2, dil3 = A_GROUPS[1][1], A_GROUPS[2][1]
    o1, l1 = o1_ref[...], l1_ref[...]
    o2, l2 = natural(o2_ref, 0, dil2), natural(l2_ref, 1, dil2)
    o3, l3 = natural(o3_ref, 2, dil3), natural(l3_ref, 3, dil3)
    mx = jnp.maximum(jnp.maximum(l1, l2), l3)
    e1, e2, e3 = jnp.exp(l1 - mx), jnp.exp(l2 - mx), jnp.exp(l3 - mx)
    attn = (e1 * o1 + e2 * o2 + e3 * o3) / (e1 + e2 + e3)


    zbuf[ZHALO:ZHALO + tm, :] = _proj(h, w_ref, OFF_CGATE, B_WIDTH) * _proj(h, w_ref, OFF_BIN, B_WIDTH)
    b_gate = _proj(h, w_ref, OFF_BGATE, B_WIDTH)
    merged = _gate(h, w_ref, 0) * _dot(attn.astype(BF16), woa_ref[...])
    conv = jnp.zeros((tm, B_WIDTH), F32)
    for k in range(B_CONV):
        start = ZHALO - (B_CONV - 1) + k
        conv = conv + cbw_ref[k:k + 1, :] * zbuf[start:start + tm, :]
    tail = zbuf[tm:tm + ZHALO, :]
    nb_ref[0] = tail
    zbuf[0:ZHALO, :] = tail

    vn = _ln(_proj(h, w_ref, OFF_GV, C_WIDTH), glng_ref[...], glnb_ref[...]).astype(BF16)
    u = _proj(h, w_ref, OFF_U, C_WIDTH)
    gate1 = _gate(h, w_ref, 1)
    merged = merged + gate1 * _dot((b_gate * conv).astype(BF16), wob_ref[...])
    gate2 = _gate(h, w_ref, 2)
    r_i = lax.broadcasted_iota(jnp.int32, (C_CHUNK, C_CHUNK), 0)
    c_i = lax.broadcasted_iota(jnp.int32, (C_CHUNK, C_CHUNK), 1)
    ws = [jnp.where(r_i >= c_i, gws_ref[g], 0.0).astype(BF16) for g in range(C_GROUPS)]
    chunks = []
    for c in range(tm // C_CHUNK):
        rows = slice(c * C_CHUNK, (c + 1) * C_CHUNK)
        chunks.append(jnp.concatenate(
            [_dot(ws[g], vn[rows, g * C_GDIM:(g + 1) * C_GDIM]) + gbc_ref[:, g:g + 1]
             for g in range(C_GROUPS)], axis=1))
    mixed = jnp.concatenate(chunks, axis=0)
    gate3 = _gate(h, w_ref, 3)
    merged = merged + gate2 * _dot((u * mixed).astype(BF16), woc_ref[...])

    dn = _ln(jnp.concatenate(dc, axis=0), dlng_ref[...], dlnb_ref[...])
    merged = merged + gate3 * _dot(jax.nn.silu(dn).astype(BF16), wod_ref[...])

    y = _dot(merged.astype(BF16), wo_ref[...])
    xo_ref[...] = x + _rms(y, post_ref[...])


def _mixer_prompt(x, o, lse, weights, gws, gb_col, layer):
    tm = MIX_TM
    steps = SEQ // tm
    row = lambda width: pl.BlockSpec((tm, width), lambda b, j: (b * steps + j, 0))
    tail = lambda n: pl.BlockSpec((1, n, B_WIDTH), lambda b, j: (b, 0, 0))
    res = lambda dil: pl.BlockSpec((1, dil, tm // dil, A_OUT), lambda b, j: (b, 0, j, 0))
    attn_specs = [row(A_OUT)] + [res(dil) for _, dil in A_GROUPS[1:]]
    flat = lambda a: a.reshape(BATCH * SEQ, A_OUT)
    return pl.pallas_call(
        functools.partial(_mixer_p_body, tm=tm),
        grid=(BATCH, steps),
        in_specs=([row(D_MODEL)] + attn_specs + attn_specs
                  + _mix_weight_specs(layer)
                  + [_of_layer((C_GROUPS, C_CHUNK, C_CHUNK), layer), _of_layer((C_CHUNK, C_GROUPS), layer)]),
        out_specs=[row(D_MODEL), tail(ZHALO), tail(GHALO)],
        out_shape=[jax.ShapeDtypeStruct((BATCH * SEQ, D_MODEL), F32),
                   jax.ShapeDtypeStruct((BATCH, ZHALO, B_WIDTH), F32),
                   jax.ShapeDtypeStruct((BATCH, GHALO, D_WIDTH), F32)],
        scratch_shapes=[pltpu.VMEM((tm + ZHALO, B_WIDTH), F32),
                        pltpu.VMEM((tm + GHALO, D_WIDTH), F32),
                        pltpu.VMEM((4 * A_OUT // LANES, tm, LANES), F32)],
        compiler_params=_cparams("parallel", "arbitrary"),
        name="mixer_prompt",
    )(x, flat(o[0]), o[1], o[2], flat(lse[0]), lse[1], lse[2], *weights, gws, gb_col)


def _mixer_s_body(x_ref, attn_ref, sb_ref, sd_ref,
                  pre_ref, post_ref, w_ref, woa_ref, wob_ref, woc_ref, wod_ref, wo_ref,
                  cbw_ref, glng_ref, glnb_ref, cdw_ref, cdb_ref, dlng_ref, dlnb_ref,
                  wtab_ref, btab_ref,
                  xo_ref, nb_ref, vn_ref, glu_ref, *, sn):
    rows = DEC_SEQ * sn
    x = x_ref[...].reshape(rows, D_MODEL)
    h = _rms(x, pre_ref[...]).astype(BF16)
    slab = lambda a, t: a[t * sn:(t + 1) * sn, :]

    merged = _gate(h, w_ref, 0) * _dot(attn_ref[...].reshape(rows, A_OUT).astype(BF16), woa_ref[...])

    zc = _proj(h, w_ref, OFF_CGATE, B_WIDTH) * _proj(h, w_ref, OFF_BIN, B_WIDTH)
    zpad = [sb_ref[i] for i in range(B_CONV - 1)] + [slab(zc, t) for t in range(DEC_SEQ)]
    conv = jnp.concatenate(
        [sum(cbw_ref[k:k + 1, :] * zpad[t + k] for k in range(B_CONV)) for t in range(DEC_SEQ)], axis=0)
    br = _dot((_proj(h, w_ref, OFF_BGATE, B_WIDTH) * conv).astype(BF16), wob_ref[...])
    merged = merged + _gate(h, w_ref, 1) * br
    for i in range(B_CONV - 1):
        nb_ref[i] = zpad[DEC_SEQ + i]

    vn = _ln(_proj(h, w_ref, OFF_GV, C_WIDTH), glng_ref[...], glnb_ref[...])
    vn_ref[...] = vn.reshape(DEC_SEQ, sn, C_WIDTH)
    mixed = []
    for t in range(DEC_SEQ):
        acc = jnp.broadcast_to(btab_ref[t:t + 1, :], (sn, C_WIDTH))
        for s in range(t + 1):
            acc = acc + wtab_ref[t * DEC_SEQ + s:t * DEC_SEQ + s + 1, :] * slab(vn, s)
        mixed.append(acc)
    mixed = jnp.concatenate(mixed, axis=0)
    br = _dot((_proj(h, w_ref, OFF_U, C_WIDTH) * mixed).astype(BF16), woc_ref[...])
    merged = merged + _gate(h, w_ref, 2) * br

    glu = _proj(h, w_ref, OFF_GLUA, D_WIDTH) * jax.nn.sigmoid(_proj(h, w_ref, OFF_GLUG, D_WIDTH))
    glu_ref[...] = glu.reshape(DEC_SEQ, sn, D_WIDTH)
    gpad = [sd_ref[i] for i in range(D_CONV - 1)] + [slab(glu, t) for t in range(DEC_SEQ)]
    dc = []
    for t in range(DEC_SEQ):
        acc = jnp.broadcast_to(cdb_ref[...], (sn, D_WIDTH))
        for k in range(D_CONV):
            acc = acc + cdw_ref[k:k + 1, :] * gpad[t + k]
        dc.append(acc)
    dn = _ln(jnp.concatenate(dc, axis=0), dlng_ref[...], dlnb_ref[...])
    br = _dot(jax.nn.silu(dn).astype(BF16), wod_ref[...])
    merged = merged + _gate(h, w_ref, 3) * br

    y = _dot(merged.astype(BF16), wo_ref[...])
    xo_ref[...] = (x + _rms(y, post_ref[...])).reshape(DEC_SEQ, sn, D_MODEL)


def _mixer_sample(x, attn, state_b, state_d, weights, wtab, btab, layer):
    sn = SAMP_SN
    tslab = lambda n, width: pl.BlockSpec((n, sn, width), lambda i: (0, i, 0))
    return pl.pallas_call(
        functools.partial(_mixer_s_body, sn=sn),
        grid=(DEC_BATCH // sn,),
        in_specs=([tslab(DEC_SEQ, D_MODEL), tslab(DEC_SEQ, A_OUT),
                   tslab(B_CONV - 1, B_WIDTH), tslab(D_CONV - 1, D_WIDTH)]
                  + _mix_weight_specs(layer)
                  + [_of_layer((DEC_SEQ * DEC_SEQ, C_WIDTH), layer), _of_layer((DEC_SEQ, C_WIDTH), layer)]),
        out_specs=[tslab(DEC_SEQ, D_MODEL), tslab(B_CONV - 1, B_WIDTH),
                   tslab(DEC_SEQ, C_WIDTH), tslab(DEC_SEQ, D_WIDTH)],
        out_shape=[jax.ShapeDtypeStruct((DEC_SEQ, DEC_BATCH, D_MODEL), F32),
                   jax.ShapeDtypeStruct((B_CONV - 1, DEC_BATCH, B_WIDTH), F32),
                   jax.ShapeDtypeStruct((DEC_SEQ, DEC_BATCH, C_WIDTH), F32),
                   jax.ShapeDtypeStruct((DEC_SEQ, DEC_BATCH, D_WIDTH), F32)],
        compiler_params=_cparams("parallel"),
        name="mixer_sample",
    )(x, attn, state_b, state_d, *weights, wtab, btab)


def kernel(x_prompt, x_sample, cache_attn_w128, cache_attn_w512, cache_attn_w2048, state_conv_b, state_conv_d, ffn1_pre_g, ffn1_post_g, ffn1_w_gate, ffn1_w_up, ffn1_w_down, mix_pre_g, mix_post_g, w_in, w_out_a, conv_b_w, w_out_b, gmlp_ln_g, gmlp_ln_b, gmlp_ws, gmlp_b, w_out_c, conv_d_w, conv_d_b, conv_d_ln_g, conv_d_ln_b, w_out_d, w_o, ffn2_pre_g, ffn2_post_g, ffn2_w_gate, ffn2_w_up, ffn2_w_down):
    bf = lambda a: a.astype(BF16)
    vec = lambda a: a.reshape(DEPTH, 1, -1)

    xp = x_prompt.reshape(BATCH * SEQ, D_MODEL)
    xs = x_sample.transpose(1, 0, 2).reshape(DEC_SEQ * DEC_BATCH, D_MODEL)
    time_minor = lambda c: c.transpose(0, 1, 3, 4, 5, 2).reshape(DEPTH, DEC_BATCH, 2 * A_OUT, c.shape[2])
    c1, c2, c3 = time_minor(cache_attn_w128), time_minor(cache_attn_w512), time_minor(cache_attn_w2048)
    state_b_tm = state_conv_b.transpose(0, 2, 1, 3)
    state_d_tm = state_conv_d.transpose(0, 2, 1, 3)

    ffn1 = (vec(ffn1_pre_g), vec(ffn1_post_g), bf(ffn1_w_gate), bf(ffn1_w_up), bf(ffn1_w_down))
    ffn2 = (vec(ffn2_pre_g), vec(ffn2_post_g), bf(ffn2_w_gate), bf(ffn2_w_up), bf(ffn2_w_down))
    w_in_bf = bf(w_in)
    pre = vec(mix_pre_g)
    mix_w = (pre, vec(mix_post_g), w_in_bf,
             bf(w_out_a), bf(w_out_b), bf(w_out_c), bf(w_out_d), bf(w_o),
             conv_b_w, vec(gmlp_ln_g), vec(gmlp_ln_b),
             conv_d_w, vec(conv_d_b), vec(conv_d_ln_g), vec(conv_d_ln_b))
    gb_col = gmlp_b.transpose(0, 2, 1)
    ws8 = gmlp_ws[:, :, :DEC_SEQ, :DEC_SEQ].transpose(0, 2, 3, 1).reshape(DEPTH, DEC_SEQ * DEC_SEQ, C_GROUPS)
    wtab = jnp.repeat(ws8, C_GDIM, axis=2)
    btab = jnp.repeat(gb_col[:, :DEC_SEQ], C_GDIM, axis=2)

    pa, sa = [[] for _ in A_GROUPS], [[] for _ in A_GROUPS]
    pb, sb, sc, pd, sd = [], [], [], [], []
    for l in range(DEPTH):
        xs = _ffn(xs, *ffn1, l)
        qs, ks, vs = _qkv_sample(xs, pre, w_in_bf, l)
        seq_major = lambda a: a.astype(F32).reshape(DEC_SEQ, DEC_BATCH, A_WIDTH).transpose(1, 0, 2)
        qs, ks, vs = seq_major(qs), seq_major(ks), seq_major(vs)
        half = DEC_BATCH // 2

        xp, attn_lo = _ffn(xp, *ffn1, l, attn=(qs, ks, vs, c1, c2, c3, 0))
        *qkv, kvt1, kvt2, kvt3 = _qkv_prompt(xp, pre, w_in_bf, l)
        outs = [_attn_prompt(qkv[g], g) for g in range(len(A_GROUPS))]
        xp, nb, nd = _mixer_prompt(xp, [o for o, _ in outs], [s for _, s in outs], mix_w, gmlp_ws, gb_col, l)
        xp, attn_hi = _ffn(xp, *ffn2, l, attn=(qs, ks, vs, c1, c2, c3, half))
        for g, kvt in enumerate((kvt1, kvt2, kvt3)):
            pa[g].append(kvt)
        pb.append(nb[:, ZHALO - (B_CONV - 1):])
        pd.append(nd[:, GHALO - (D_CONV - 1):])

        attn_s = jnp.concatenate([attn_lo, attn_hi], axis=0).transpose(1, 0, 2)
        xs3, nb_s, vn_s, glu_s = _mixer_sample(
            xs.reshape(DEC_SEQ, DEC_BATCH, D_MODEL), attn_s, state_b_tm[l], state_d_tm[l], mix_w, wtab, btab, l)
        xs = _ffn(xs3.reshape(DEC_SEQ * DEC_BATCH, D_MODEL), *ffn2, l)
        k4 = ks.reshape(DEC_BATCH, DEC_SEQ, A_HEADS, HEAD_DIM)
        v4 = vs.reshape(DEC_BATCH, DEC_SEQ, A_HEADS, HEAD_DIM)
        for g in range(len(A_GROUPS)):
            hs = slice(g * A_HPG, (g + 1) * A_HPG)
            sa[g].append(jnp.stack([k4[:, :, hs], v4[:, :, hs]], axis=2))
        sb.append(nb_s.transpose(1, 0, 2))
        sc.append(vn_s.transpose(1, 0, 2))
        sd.append(jnp.concatenate([state_conv_d[l][:, DEC_SEQ:], glu_s.transpose(1, 0, 2)], axis=1))

    st = lambda xs_: jnp.stack(xs_, axis=0)
    new_kv = lambda ts: st(ts).reshape(DEPTH, BATCH, 2, A_HPG, HEAD_DIM, -1).transpose(0, 1, 5, 2, 3, 4)
    return (xp.reshape(BATCH, SEQ, D_MODEL),
            xs.reshape(DEC_SEQ, DEC_BATCH, D_MODEL).transpose(1, 0, 2),
            new_kv(pa[0]), new_kv(pa[1]), new_kv(pa[2]), st(sa[0]), st(sa[1]), st(sa[2]),
            st(pb), st(sb), st(sc), st(pd), st(sd))
```

```python
import functools

import numpy as np
import jax
import jax.numpy as jnp
from jax import lax
from jax.experimental import pallas as pl
from jax.experimental.pallas import tpu as pltpu

D_MODEL = 1024
BATCH = 4
SEQ = 4096
DEPTH = 2
DEC_BATCH = 128
DEC_SEQ = 8
HEAD_DIM = 64
A_GROUPS = ((128, 1), (512, 4), (2048, 16))
A_HPG = 4
A_HEADS = A_HPG * len(A_GROUPS)
A_WIDTH = A_HEADS * HEAD_DIM
A_OUT = A_HPG * HEAD_DIM
NK = 128
B_WIDTH = 512
B_CONV = 3
C_WIDTH = 512
C_CHUNK = 128
C_GROUPS = 4
C_GDIM = 128
D_WIDTH = 512
D_CONV = 31
D_FF = 2816
EPS = 1e-6
QKV_COLS = 3 * A_WIDTH
OFF_BGATE, OFF_CGATE, OFF_BIN, OFF_U, OFF_GV, OFF_GLUA, OFF_GLUG, OFF_GATES = (
    0, 512, 1024, 1536, 2048, 2560, 3072, 3584)
REST_COLS = OFF_GATES + 4 * D_MODEL

F32 = jnp.float32
BF16 = jnp.bfloat16
NEG = -1e30
SUBLANES = 8
LANES = 128
V7X_VMEM_LIMIT_BYTES = 56 * 1024 * 1024

FFN_TM = 512
FFN_FC = 256
QKV_TM = 1024
QKV_S_TM = 512
MIX_TM = 512
MIX_RC = 64
ZHALO = 8
GHALO = 32
ATT_TQ = 1024
SAMP_SB = 2
SAMP_SN = 64


def _cparams(*sem):
    return pltpu.CompilerParams(dimension_semantics=sem, vmem_limit_bytes=V7X_VMEM_LIMIT_BYTES)


def _resident(shape):
    nd = len(shape)
    return pl.BlockSpec(shape, lambda *_: (0,) * nd, pipeline_mode=pl.Buffered(1))


def _of_layer(shape, layer):
    nd = len(shape)
    return pl.BlockSpec((None,) + tuple(shape), lambda *_: (layer,) + (0,) * nd, pipeline_mode=pl.Buffered(1))


def _dot(a, b):
    return jnp.dot(a, b, preferred_element_type=F32)


def _dot_nt(a, b):
    return lax.dot_general(a, b, (((1,), (1,)), ((), ())), preferred_element_type=F32)


def _rms(x, g):
    return x * lax.rsqrt(jnp.mean(x * x, axis=-1, keepdims=True) + EPS) * g


def _ln(x, g, b):
    mu = jnp.mean(x, axis=-1, keepdims=True)
    xc = x - mu
    var = jnp.mean(xc * xc, axis=-1, keepdims=True)
    return xc * lax.rsqrt(var + EPS) * g + b


def _ffn_body(x_ref, pre_ref, post_ref, wg_ref, wu_ref, wd_ref, *rest, sb):
    if sb:
        q_ref, k_ref, v_ref, c1_ref, c2_ref, c3_ref, bias_ref, o_ref, attn_ref = rest
        caches = (c1_ref, c2_ref, c3_ref)
        staged = [_samp_scores(n, q_ref, k_ref, v_ref, caches, bias_ref) for n in range(sb)]
        staged = [_samp_probs(scores) + (new_v,) for scores, new_v in staged]
    else:
        (o_ref,) = rest
    x = x_ref[...]
    h = _rms(x, pre_ref[...]).astype(BF16)
    acc = jnp.zeros(x.shape, F32)
    chunks = D_FF // FFN_FC
    for c in range(chunks):
        if sb and c == chunks // 2:
            for n, (probs, den, new_v) in enumerate(staged):
                attn_ref[n] = _samp_out(n, probs, den, new_v, caches)
        sl = slice(c * FFN_FC, (c + 1) * FFN_FC)
        g = _dot(h, wg_ref[:, sl])
        u = _dot(h, wu_ref[:, sl])
        acc = acc + _dot((jax.nn.silu(g) * u).astype(BF16), wd_ref[sl, :])
    o_ref[...] = x + 0.5 * _rms(acc, post_ref[...])


def _ffn(x, pre_g, post_g, wg, wu, wd, layer, attn=None):
    rows = x.shape[0]
    tm = min(FFN_TM, rows)
    steps = rows // tm
    in_specs = [pl.BlockSpec((tm, D_MODEL), lambda i: (i, 0)),
                _of_layer((1, D_MODEL), layer), _of_layer((1, D_MODEL), layer),
                _of_layer((D_MODEL, D_FF), layer), _of_layer((D_MODEL, D_FF), layer),
                _of_layer((D_FF, D_MODEL), layer)]
    out_specs = [pl.BlockSpec((tm, D_MODEL), lambda i: (i, 0))]
    out_shape = [jax.ShapeDtypeStruct((rows, D_MODEL), F32)]
    operands = [x, pre_g, post_g, wg, wu, wd]
    sb = 0
    if attn is not None:
        *operands_a, first_seq = attn
        sb = SAMP_SB
        blk0 = first_seq // sb
        qspec = pl.BlockSpec((sb, DEC_SEQ, A_WIDTH), lambda i: (i + blk0, 0, 0))
        cspec = lambda n: pl.BlockSpec((1, sb, 2 * A_OUT, n), lambda i: (layer, i + blk0, 0, 0))
        in_specs += [qspec, qspec, qspec] + [cspec(n) for n in SAMP_CACHE_LEN] + [_resident((SAMP_Q, SAMP_KEYS))]
        out_specs.append(pl.BlockSpec((sb, DEC_SEQ, A_OUT), lambda i: (i, 0, 0)))
        out_shape.append(jax.ShapeDtypeStruct((steps * sb, DEC_SEQ, A_OUT), F32))
        operands += operands_a + [jnp.asarray(_sample_bias())]
    out = pl.pallas_call(
        functools.partial(_ffn_body, sb=sb),
        grid=(steps,),
        in_specs=in_specs,
        out_specs=out_specs,
        out_shape=out_shape,
        compiler_params=_cparams("parallel"),
        name="ffn_attn" if sb else "ffn",
    )(*operands)
    return out if sb else out[0]


Q_SCALE = HEAD_DIM ** -0.5


def _qkv_s_body(x_ref, g_ref, w_ref, q_ref, k_ref, v_ref):
    h = _rms(x_ref[...], g_ref[...]).astype(BF16)
    q_ref[...] = (_dot(h, w_ref[:, 0:A_WIDTH]) * Q_SCALE).astype(BF16)
    k_ref[...] = _dot(h, w_ref[:, A_WIDTH:2 * A_WIDTH]).astype(BF16)
    v_ref[...] = _dot(h, w_ref[:, 2 * A_WIDTH:3 * A_WIDTH]).astype(BF16)


def _qkv_sample(x, g, w_qkv, layer):
    rows = x.shape[0]
    tm = min(QKV_S_TM, rows)
    out = jax.ShapeDtypeStruct((rows, A_WIDTH), BF16)
    spec = pl.BlockSpec((tm, A_WIDTH), lambda i: (i, 0))
    return pl.pallas_call(
        _qkv_s_body,
        grid=(rows // tm,),
        in_specs=[pl.BlockSpec((tm, D_MODEL), lambda i: (i, 0)),
                  _of_layer((1, D_MODEL), layer), _of_layer((D_MODEL, QKV_COLS), layer)],
        out_specs=[spec, spec, spec],
        out_shape=[out, out, out],
        compiler_params=_cparams("parallel"),
        name="qkv_sample",
    )(x, g, w_qkv)


def _kv_tail(group, tm):
    keep = min(A_GROUPS[group][0], SEQ)
    rows = min(keep, tm)
    return (SEQ - keep) // tm, rows, tm - rows


def _qkv_p_body(x_ref, g_ref, w_ref, o1_ref, o2_ref, o3_ref, t1_ref, t2_ref, t3_ref, acc_ref, *, tm):
    h = _rms(x_ref[...], g_ref[...]).astype(BF16)
    for g, (_, dil) in enumerate(A_GROUPS):
        for part in range(3):
            src = part * A_WIDTH + g * A_OUT
            res = _dot(h, w_ref[:, src:src + A_OUT])
            for sub in range(A_OUT // LANES):
                acc_ref[src // LANES + sub] = res[:, sub * LANES:(sub + 1) * LANES]
        out = (o1_ref, o2_ref, o3_ref)[g]
        for c in range(3 * A_OUT // LANES):
            part, sub = divmod(c * LANES, A_OUT)
            src = (part * A_WIDTH + g * A_OUT + sub) // LANES
            for r in range(dil):
                val = acc_ref[src] if dil == 1 else acc_ref[src, pl.ds(r, tm // dil, stride=dil), :]
                if part == 0:
                    val = val * Q_SCALE
                out[0, r, :, c * LANES:(c + 1) * LANES] = val.astype(BF16)
    for g in range(len(A_GROUPS)):
        first, rows, row0 = _kv_tail(g, tm)
        out_t = (t1_ref, t2_ref, t3_ref)[g]

        @pl.when(pl.program_id(1) >= first)
        def _(g=g, rows=rows, row0=row0, out_t=out_t):
            for c in range(2 * A_OUT // LANES):
                part, sub = divmod(c * LANES, A_OUT)
                src = ((part + 1) * A_WIDTH + g * A_OUT + sub) // LANES
                out_t[0, c * LANES:(c + 1) * LANES, :] = acc_ref[src, row0:row0 + rows, :].T


def _qkv_prompt(x, g, w_qkv, layer):
    tm = QKV_TM
    steps = SEQ // tm
    dils = [dil for _, dil in A_GROUPS]
    tails = [_kv_tail(g, tm) for g in range(len(A_GROUPS))]
    tail_spec = lambda first, rows: pl.BlockSpec((1, 2 * A_OUT, rows), lambda b, j: (b, 0, jnp.maximum(j - first, 0)))
    return pl.pallas_call(
        functools.partial(_qkv_p_body, tm=tm),
        grid=(BATCH, steps),
        in_specs=[pl.BlockSpec((tm, D_MODEL), lambda b, j: (b * steps + j, 0)),
                  _of_layer((1, D_MODEL), layer), _of_layer((D_MODEL, QKV_COLS), layer)],
        out_specs=([pl.BlockSpec((1, dil, tm // dil, 3 * A_OUT), lambda b, j: (b, 0, j, 0)) for dil in dils]
                   + [tail_spec(first, rows) for first, rows, _ in tails]),
        out_shape=([jax.ShapeDtypeStruct((BATCH, dil, SEQ // dil, 3 * A_OUT), BF16) for dil in dils]
                   + [jax.ShapeDtypeStruct((BATCH, 2 * A_OUT, min(win, SEQ)), F32) for win, _ in A_GROUPS]),
        scratch_shapes=[pltpu.VMEM((QKV_COLS // LANES, tm, LANES), F32)],
        compiler_params=_cparams("parallel", "arbitrary"),
        name="qkv_prompt",
    )(x, g, w_qkv)


def _alibi_slopes():
    return np.exp2(-8.0 * np.arange(1, A_HEADS + 1, dtype=np.float64) / A_HEADS)


def _prompt_bias(group):
    _, dil = A_GROUPS[group]
    rel = NK + np.arange(NK)[:, None] - np.arange(2 * NK)[None, :]
    valid = (rel >= 0) & (rel <= NK)
    slopes = _alibi_slopes()[group * A_HPG:(group + 1) * A_HPG]
    bias = -slopes[:, None, None] * (dil * rel)[None].astype(np.float64)
    return np.where(valid[None], bias, NEG).astype(np.float32).reshape(A_HPG * NK, 2 * NK)


def _attn_p_body(q_ref, kp_ref, kc_ref, vp_ref, vc_ref, bias_ref, o_ref, lse_ref, *, tq, rb):
    first = (pl.program_id(2) == 0).astype(F32)
    lane_head = lax.broadcasted_iota(jnp.int32, (1, A_OUT), 1) // HEAD_DIM
    key_col = lax.broadcasted_iota(jnp.int32, (1, 2 * NK), 1)
    no_prev = jnp.where(key_col < NK, NEG, 0.0) * first
    blocks = [(r, j) for r in range(rb) for j in range(tq // NK)]
    blk = lambda ref, r, j: ref[0, r, j * NK:(j + 1) * NK, :]
    scores, values = [], []
    for r, j in blocks:
        qj = blk(q_ref, r, j)
        qs = jnp.concatenate([jnp.where(lane_head == h, qj, jnp.zeros_like(qj)) for h in range(A_HPG)], axis=0)
        k_prev, v_prev = ((kp_ref[0, r], vp_ref[0, r]) if j == 0
                          else (blk(kc_ref, r, j - 1), blk(vc_ref, r, j - 1)))
        s = _dot_nt(qs, jnp.concatenate([k_prev, blk(kc_ref, r, j)], axis=0)) + bias_ref[...]
        scores.append(s + no_prev if j == 0 else s)
        values.append(jnp.concatenate([v_prev, blk(vc_ref, r, j)], axis=0))
    stats = []
    for i in range(len(blocks)):
        m = jnp.max(scores[i], axis=-1, keepdims=True)
        p = jnp.exp(scores[i] - m)
        den = jnp.sum(p, axis=-1, keepdims=True)
        scores[i] = p.astype(BF16)
        stats.append((den, m + jnp.log(den)))
    for i, (r, j) in enumerate(blocks):
        den, lse = stats[i]
        pv = _dot(scores[i], values[i]) / den
        o_acc = jnp.zeros((NK, A_OUT), F32)
        l_acc = jnp.zeros((NK, A_OUT), F32)
        for h in range(A_HPG):
            rows = slice(h * NK, (h + 1) * NK)
            o_acc = jnp.where(lane_head == h, pv[rows], o_acc)
            l_acc = jnp.where(lane_head == h, lse[rows], l_acc)
        o_ref[0, r, j * NK:(j + 1) * NK, :] = o_acc
        lse_ref[0, r, j * NK:(j + 1) * NK, :] = l_acc


def _attn_prompt(qkv, group):
    _, dil = A_GROUPS[group]
    sub = SEQ // dil
    tq = min(ATT_TQ, sub)
    rb = min(dil, ATT_TQ // tq)
    cur = lambda col: pl.BlockSpec((1, rb, tq, A_OUT), lambda b, r, i: (b, r, i, col))
    prev = lambda col: pl.BlockSpec((1, rb, NK, A_OUT),
                                    lambda b, r, i: (b, r, jnp.maximum(i * (tq // NK) - 1, 0), col))
    out = jax.ShapeDtypeStruct((BATCH, dil, sub, A_OUT), F32)
    return pl.pallas_call(
        functools.partial(_attn_p_body, tq=tq, rb=rb),
        grid=(BATCH, dil // rb, sub // tq),
        in_specs=[cur(0), prev(1), cur(1), prev(2), cur(2), _resident((A_HPG * NK, 2 * NK))],
        out_specs=[cur(0), cur(0)],
        out_shape=[out, out],
        compiler_params=_cparams("parallel", "parallel", "arbitrary"),
        name=f"attn_prompt_g{group}",
    )(qkv, qkv, qkv, qkv, qkv, jnp.asarray(_prompt_bias(group)))


SAMP_CACHE_LEN = tuple(win for win, _ in A_GROUPS)
SAMP_NEW_SLOT = 128
SAMP_SEG = []
_off = 0
for _len in SAMP_CACHE_LEN:
    SAMP_SEG.append((_off, _off + _len, _off + _len + SAMP_NEW_SLOT))
    _off += _len + SAMP_NEW_SLOT
SAMP_KEYS = _off
SAMP_Q = A_HPG * DEC_SEQ


def _sample_bias():
    slopes = _alibi_slopes()
    bias = np.full((SAMP_Q, SAMP_KEYS), NEG, np.float64)
    for c in range(SAMP_Q):
        h, t = divmod(c, DEC_SEQ)
        for g, (win, dil) in enumerate(A_GROUPS):
            c0, n0, _ = SAMP_SEG[g]
            sl = slopes[g * A_HPG + h]
            dist = win + t - np.arange(win)
            ok = (dist % dil == 0) & (dist <= win)
            bias[c, c0:c0 + win] = np.where(ok, -sl * dist, NEG)
            for t2 in range(t + 1):
                if (t - t2) % dil == 0:
                    bias[c, n0 + t2] = -sl * (t - t2)
    return bias.astype(np.float32)


def _samp_scores(n, q_ref, k_ref, v_ref, caches, bias_ref):
    lane_head = lax.broadcasted_iota(jnp.int32, (1, A_OUT), 1) // HEAD_DIM
    qn, kn, vn = q_ref[n], k_ref[n], v_ref[n]
    pad = jnp.zeros((SAMP_NEW_SLOT - DEC_SEQ, A_OUT), F32)
    scores, new_v = [], []
    for g in range(len(A_GROUPS)):
        c0, n0, e0 = SAMP_SEG[g]
        cols = slice(g * A_OUT, (g + 1) * A_OUT)
        k_new = jnp.concatenate([kn[:, cols], pad], axis=0).astype(BF16)
        new_v.append(jnp.concatenate([vn[:, cols], pad], axis=0).astype(BF16))
        qbd = jnp.concatenate([jnp.where(lane_head == h, qn[:, cols], 0.0) for h in range(A_HPG)],
                              axis=0).astype(BF16)
        k_t = caches[g][0, n, 0:A_OUT, :].astype(BF16)
        scores.append(_dot(qbd, k_t) + bias_ref[:, c0:n0])
        scores.append(_dot_nt(qbd, k_new) + bias_ref[:, n0:e0])
    return scores, new_v


def _samp_probs(scores):
    m = functools.reduce(jnp.maximum, [jnp.max(s, axis=-1, keepdims=True) for s in scores])
    probs = [jnp.exp(s - m) for s in scores]
    den = functools.reduce(jnp.add, [jnp.sum(p, axis=-1, keepdims=True) for p in probs])
    return [p.astype(BF16) for p in probs], den


def _samp_out(n, probs, den, new_v, caches):
    lane_head = lax.broadcasted_iota(jnp.int32, (1, A_OUT), 1) // HEAD_DIM
    out = jnp.zeros((SAMP_Q, A_OUT), F32)
    for g in range(len(A_GROUPS)):
        v_t = caches[g][0, n, A_OUT:2 * A_OUT, :].astype(BF16)
        out = out + _dot_nt(probs[2 * g], v_t) + _dot(probs[2 * g + 1], new_v[g])
    out = out / den
    attn = jnp.zeros((DEC_SEQ, A_OUT), F32)
    for h in range(A_HPG):
        attn = jnp.where(lane_head == h, out[h * DEC_SEQ:(h + 1) * DEC_SEQ, :], attn)
    return attn


def _proj(h, w_ref, off, width):
    return _dot(h, w_ref[0, :, off:off + width])


def _gate(h, w_ref, branch):
    return jax.nn.sigmoid(_proj(h, w_ref, OFF_GATES + branch * D_MODEL, D_MODEL))


_MIX_WEIGHT_SHAPES = (
    (1, D_MODEL), (1, D_MODEL),
    (D_MODEL, REST_COLS),
    (A_OUT, D_MODEL), (B_WIDTH, D_MODEL), (C_WIDTH, D_MODEL), (D_WIDTH, D_MODEL), (D_MODEL, D_MODEL),
    (B_CONV, B_WIDTH), (1, C_WIDTH), (1, C_WIDTH),
    (D_CONV, D_WIDTH), (1, D_WIDTH), (1, D_WIDTH), (1, D_WIDTH),
)
_MIX_W_IN = 2


def _mix_weight_specs(layer):
    specs = [_of_layer(s, layer) for s in _MIX_WEIGHT_SHAPES]
    specs[_MIX_W_IN] = pl.BlockSpec((pl.Element(1), pl.Element(D_MODEL), pl.Element(REST_COLS)),
                                    lambda *_: (layer, 0, QKV_COLS), pipeline_mode=pl.Buffered(1))
    return specs


def _mixer_p_body(x_ref, o1_ref, o2_ref, o3_ref, l1_ref, l2_ref, l3_ref,
                  pre_ref, post_ref, w_ref, woa_ref, wob_ref, woc_ref, wod_ref, wo_ref,
                  cbw_ref, glng_ref, glnb_ref, cdw_ref, cdb_ref, dlng_ref, dlnb_ref,
                  gws_ref, gbc_ref,
                  xo_ref, nb_ref, nd_ref, zbuf, gbuf, nat, *, tm):
    @pl.when(pl.program_id(1) == 0)
    def _():
        zbuf[0:ZHALO, :] = jnp.zeros((ZHALO, B_WIDTH), F32)
        gbuf[0:GHALO, :] = jnp.zeros((GHALO, D_WIDTH), F32)

    x = x_ref[...]
    h = _rms(x, pre_ref[...]).astype(BF16)

    gbuf[GHALO:GHALO + tm, :] = (_proj(h, w_ref, OFF_GLUA, D_WIDTH)
                                 * jax.nn.sigmoid(_proj(h, w_ref, OFF_GLUG, D_WIDTH)))
    wrows = MIX_RC + GHALO
    dc = []
    for ci in range(tm // MIX_RC):
        win = gbuf[ci * MIX_RC:ci * MIX_RC + wrows, :]
        phase = [win] + [pltpu.roll(win, wrows - s, axis=0) for s in range(1, SUBLANES)]
        acc = jnp.broadcast_to(cdb_ref[...], (MIX_RC, D_WIDTH))
        for k in range(D_CONV):
            a, s = divmod(GHALO - (D_CONV - 1) + k, SUBLANES)
            acc = acc + cdw_ref[k:k + 1, :] * phase[s][a * SUBLANES:a * SUBLANES + MIX_RC, :]
        dc.append(acc)
    tail = gbuf[tm:tm + GHALO, :]
    nd_ref[0] = tail
    gbuf[0:GHALO, :] = tail

    def natural(ref, slot, dil):
        halves = A_OUT // LANES
        for r in range(dil):
            for c in range(halves):
                nat[slot * halves + c, pl.ds(r, tm // dil, stride=dil), :] = ref[0, r, :, c * LANES:(c + 1) * LANES]
        return jnp.concatenate([nat[slot * halves + c] for c in range(halves)], axis=1)

    dil2, dil3 = A_GROUPS[1][1], A_GROUPS[2][1]
    o1, l1 = o1_ref[...], l1_ref[...]
    o2, l2 = natural(o2_ref, 0, dil2), natural(l2_ref, 1, dil2)
    o3, l3 = natural(o3_ref, 2, dil3), natural(l3_ref, 3, dil3)
    mx = jnp.maximum(jnp.maximum(l1, l2), l3)
    e1, e2, e3 = jnp.exp(l1 - mx), jnp.exp(l2 - mx), jnp.exp(l3 - mx)
    attn = (e1 * o1 + e2 * o2 + e3 * o3) / (e1 + e2 + e3)


    zbuf[ZHALO:ZHALO + tm, :] = _proj(h, w_ref, OFF_CGATE, B_WIDTH) * _proj(h, w_ref, OFF_BIN, B_WIDTH)
    b_gate = _proj(h, w_ref, OFF_BGATE, B_WIDTH)
    merged = _gate(h, w_ref, 0) * _dot(attn.astype(BF16), woa_ref[...])
    conv = jnp.zeros((tm, B_WIDTH), F32)
    for k in range(B_CONV):
        start = ZHALO - (B_CONV - 1) + k
        conv = conv + cbw_ref[k:k + 1, :] * zbuf[start:start + tm, :]
    tail = zbuf[tm:tm + ZHALO, :]
    nb_ref[0] = tail
    zbuf[0:ZHALO, :] = tail

    vn = _ln(_proj(h, w_ref, OFF_GV, C_WIDTH), glng_ref[...], glnb_ref[...]).astype(BF16)
    u = _proj(h, w_ref, OFF_U, C_WIDTH)
    gate1 = _gate(h, w_ref, 1)
    merged = merged + gate1 * _dot((b_gate * conv).astype(BF16), wob_ref[...])
    gate2 = _gate(h, w_ref, 2)
    r_i = lax.broadcasted_iota(jnp.int32, (C_CHUNK, C_CHUNK), 0)
    c_i = lax.broadcasted_iota(jnp.int32, (C_CHUNK, C_CHUNK), 1)
    ws = [jnp.where(r_i >= c_i, gws_ref[g], 0.0).astype(BF16) for g in range(C_GROUPS)]
    chunks = []
    for c in range(tm // C_CHUNK):
        rows = slice(c * C_CHUNK, (c + 1) * C_CHUNK)
        chunks.append(jnp.concatenate(
            [_dot(ws[g], vn[rows, g * C_GDIM:(g + 1) * C_GDIM]) + gbc_ref[:, g:g + 1]
             for g in range(C_GROUPS)], axis=1))
    mixed = jnp.concatenate(chunks, axis=0)
    gate3 = _gate(h, w_ref, 3)
    merged = merged + gate2 * _dot((u * mixed).astype(BF16), woc_ref[...])

    dn = _ln(jnp.concatenate(dc, axis=0), dlng_ref[...], dlnb_ref[...])
    merged = merged + gate3 * _dot(jax.nn.silu(dn).astype(BF16), wod_ref[...])

    y = _dot(merged.astype(BF16), wo_ref[...])
    xo_ref[...] = x + _rms(y, post_ref[...])


def _mixer_prompt(x, o, lse, weights, gws, gb_col, layer):
    tm = MIX_TM
    steps = SEQ // tm
    row = lambda width: pl.BlockSpec((tm, width), lambda b, j: (b * steps + j, 0))
    tail = lambda n: pl.BlockSpec((1, n, B_WIDTH), lambda b, j: (b, 0, 0))
    res = lambda dil: pl.BlockSpec((1, dil, tm // dil, A_OUT), lambda b, j: (b, 0, j, 0))
    attn_specs = [row(A_OUT)] + [res(dil) for _, dil in A_GROUPS[1:]]
    flat = lambda a: a.reshape(BATCH * SEQ, A_OUT)
    return pl.pallas_call(
        functools.partial(_mixer_p_body, tm=tm),
        grid=(BATCH, steps),
        in_specs=([row(D_MODEL)] + attn_specs + attn_specs
                  + _mix_weight_specs(layer)
                  + [_of_layer((C_GROUPS, C_CHUNK, C_CHUNK), layer), _of_layer((C_CHUNK, C_GROUPS), layer)]),
        out_specs=[row(D_MODEL), tail(ZHALO), tail(GHALO)],
        out_shape=[jax.ShapeDtypeStruct((BATCH * SEQ, D_MODEL), F32),
                   jax.ShapeDtypeStruct((BATCH, ZHALO, B_WIDTH), F32),
                   jax.ShapeDtypeStruct((BATCH, GHALO, D_WIDTH), F32)],
        scratch_shapes=[pltpu.VMEM((tm + ZHALO, B_WIDTH), F32),
                        pltpu.VMEM((tm + GHALO, D_WIDTH), F32),
                        pltpu.VMEM((4 * A_OUT // LANES, tm, LANES), F32)],
        compiler_params=_cparams("parallel", "arbitrary"),
        name="mixer_prompt",
    )(x, flat(o[0]), o[1], o[2], flat(lse[0]), lse[1], lse[2], *weights, gws, gb_col)


def _mixer_s_body(x_ref, attn_ref, sb_ref, sd_ref,
                  pre_ref, post_ref, w_ref, woa_ref, wob_ref, woc_ref, wod_ref, wo_ref,
                  cbw_ref, glng_ref, glnb_ref, cdw_ref, cdb_ref, dlng_ref, dlnb_ref,
                  wtab_ref, btab_ref,
                  xo_ref, nb_ref, vn_ref, glu_ref, *, sn):
    rows = DEC_SEQ * sn
    x = x_ref[...].reshape(rows, D_MODEL)
    h = _rms(x, pre_ref[...]).astype(BF16)
    slab = lambda a, t: a[t * sn:(t + 1) * sn, :]

    merged = _gate(h, w_ref, 0) * _dot(attn_ref[...].reshape(rows, A_OUT).astype(BF16), woa_ref[...])

    zc = _proj(h, w_ref, OFF_CGATE, B_WIDTH) * _proj(h, w_ref, OFF_BIN, B_WIDTH)
    zpad = [sb_ref[i] for i in range(B_CONV - 1)] + [slab(zc, t) for t in range(DEC_SEQ)]
    conv = jnp.concatenate(
        [sum(cbw_ref[k:k + 1, :] * zpad[t + k] for k in range(B_CONV)) for t in range(DEC_SEQ)], axis=0)
    br = _dot((_proj(h, w_ref, OFF_BGATE, B_WIDTH) * conv).astype(BF16), wob_ref[...])
    merged = merged + _gate(h, w_ref, 1) * br
    for i in range(B_CONV - 1):
        nb_ref[i] = zpad[DEC_SEQ + i]

    vn = _ln(_proj(h, w_ref, OFF_GV, C_WIDTH), glng_ref[...], glnb_ref[...])
    vn_ref[...] = vn.reshape(DEC_SEQ, sn, C_WIDTH)
    mixed = []
    for t in range(DEC_SEQ):
        acc = jnp.broadcast_to(btab_ref[t:t + 1, :], (sn, C_WIDTH))
        for s in range(t + 1):
            acc = acc + wtab_ref[t * DEC_SEQ + s:t * DEC_SEQ + s + 1, :] * slab(vn, s)
        mixed.append(acc)
    mixed = jnp.concatenate(mixed, axis=0)
    br = _dot((_proj(h, w_ref, OFF_U, C_WIDTH) * mixed).astype(BF16), woc_ref[...])
    merged = merged + _gate(h, w_ref, 2) * br

    glu = _proj(h, w_ref, OFF_GLUA, D_WIDTH) * jax.nn.sigmoid(_proj(h, w_ref, OFF_GLUG, D_WIDTH))
    glu_ref[...] = glu.reshape(DEC_SEQ, sn, D_WIDTH)
    gpad = [sd_ref[i] for i in range(D_CONV - 1)] + [slab(glu, t) for t in range(DEC_SEQ)]
    dc = []
    for t in range(DEC_SEQ):
        acc = jnp.broadcast_to(cdb_ref[...], (sn, D_WIDTH))
        for k in range(D_CONV):
            acc = acc + cdw_ref[k:k + 1, :] * gpad[t + k]
        dc.append(acc)
    dn = _ln(jnp.concatenate(dc, axis=0), dlng_ref[...], dlnb_ref[...])
    br = _dot(jax.nn.silu(dn).astype(BF16), wod_ref[...])
    merged = merged + _gate(h, w_ref, 3) * br

    y = _dot(merged.astype(BF16), wo_ref[...])
    xo_ref[...] = (x + _rms(y, post_ref[...])).reshape(DEC_SEQ, sn, D_MODEL)


def _mixer_sample(x, attn, state_b, state_d, weights, wtab, btab, layer):
    sn = SAMP_SN
    tslab = lambda n, width: pl.BlockSpec((n, sn, width), lambda i: (0, i, 0))
    return pl.pallas_call(
        functools.partial(_mixer_s_body, sn=sn),
        grid=(DEC_BATCH // sn,),
        in_specs=([tslab(DEC_SEQ, D_MODEL), tslab(DEC_SEQ, A_OUT),
                   tslab(B_CONV - 1, B_WIDTH), tslab(D_CONV - 1, D_WIDTH)]
                  + _mix_weight_specs(layer)
                  + [_of_layer((DEC_SEQ * DEC_SEQ, C_WIDTH), layer), _of_layer((DEC_SEQ, C_WIDTH), layer)]),
        out_specs=[tslab(DEC_SEQ, D_MODEL), tslab(B_CONV - 1, B_WIDTH),
                   tslab(DEC_SEQ, C_WIDTH), tslab(DEC_SEQ, D_WIDTH)],
        out_shape=[jax.ShapeDtypeStruct((DEC_SEQ, DEC_BATCH, D_MODEL), F32),
                   jax.ShapeDtypeStruct((B_CONV - 1, DEC_BATCH, B_WIDTH), F32),
                   jax.ShapeDtypeStruct((DEC_SEQ, DEC_BATCH, C_WIDTH), F32),
                   jax.ShapeDtypeStruct((DEC_SEQ, DEC_BATCH, D_WIDTH), F32)],
        compiler_params=_cparams("parallel"),
        name="mixer_sample",
    )(x, attn, state_b, state_d, *weights, wtab, btab)


def kernel(x_prompt, x_sample, cache_attn_w128, cache_attn_w512, cache_attn_w2048, state_conv_b, state_conv_d, ffn1_pre_g, ffn1_post_g, ffn1_w_gate, ffn1_w_up, ffn1_w_down, mix_pre_g, mix_post_g, w_in, w_out_a, conv_b_w, w_out_b, gmlp_ln_g, gmlp_ln_b, gmlp_ws, gmlp_b, w_out_c, conv_d_w, conv_d_b, conv_d_ln_g, conv_d_ln_b, w_out_d, w_o, ffn2_pre_g, ffn2_post_g, ffn2_w_gate, ffn2_w_up, ffn2_w_down):
    bf = lambda a: a.astype(BF16)
    vec = lambda a: a.reshape(DEPTH, 1, -1)

    xp = x_prompt.reshape(BATCH * SEQ, D_MODEL)
    xs = x_sample.transpose(1, 0, 2).reshape(DEC_SEQ * DEC_BATCH, D_MODEL)
    time_minor = lambda c: c.transpose(0, 1, 3, 4, 5, 2).reshape(DEPTH, DEC_BATCH, 2 * A_OUT, c.shape[2])
    c1, c2, c3 = time_minor(cache_attn_w128), time_minor(cache_attn_w512), time_minor(cache_attn_w2048)
    state_b_tm = state_conv_b.transpose(0, 2, 1, 3)
    state_d_tm = state_conv_d.transpose(0, 2, 1, 3)

    ffn1 = (vec(ffn1_pre_g), vec(ffn1_post_g), bf(ffn1_w_gate), bf(ffn1_w_up), bf(ffn1_w_down))
    ffn2 = (vec(ffn2_pre_g), vec(ffn2_post_g), bf(ffn2_w_gate), bf(ffn2_w_up), bf(ffn2_w_down))
    w_in_bf = bf(w_in)
    pre = vec(mix_pre_g)
    mix_w = (pre, vec(mix_post_g), w_in_bf,
             bf(w_out_a), bf(w_out_b), bf(w_out_c), bf(w_out_d), bf(w_o),
             conv_b_w, vec(gmlp_ln_g), vec(gmlp_ln_b),
             conv_d_w, vec(conv_d_b), vec(conv_d_ln_g), vec(conv_d_ln_b))
    gb_col = gmlp_b.transpose(0, 2, 1)
    ws8 = gmlp_ws[:, :, :DEC_SEQ, :DEC_SEQ].transpose(0, 2, 3, 1).reshape(DEPTH, DEC_SEQ * DEC_SEQ, C_GROUPS)
    wtab = jnp.repeat(ws8, C_GDIM, axis=2)
    btab = jnp.repeat(gb_col[:, :DEC_SEQ], C_GDIM, axis=2)

    pa, sa = [[] for _ in A_GROUPS], [[] for _ in A_GROUPS]
    pb, sb, sc, pd, sd = [], [], [], [], []
    for l in range(DEPTH):
        xs = _ffn(xs, *ffn1, l)
        qs, ks, vs = _qkv_sample(xs, pre, w_in_bf, l)
        seq_major = lambda a: a.astype(F32).reshape(DEC_SEQ, DEC_BATCH, A_WIDTH).transpose(1, 0, 2)
        qs, ks, vs = seq_major(qs), seq_major(ks), seq_major(vs)
        half = DEC_BATCH // 2

        xp, attn_lo = _ffn(xp, *ffn1, l, attn=(qs, ks, vs, c1, c2, c3, 0))
        *qkv, kvt1, kvt2, kvt3 = _qkv_prompt(xp, pre, w_in_bf, l)
        outs = [_attn_prompt(qkv[g], g) for g in range(len(A_GROUPS))]
        xp, nb, nd = _mixer_prompt(xp, [o for o, _ in outs], [s for _, s in outs], mix_w, gmlp_ws, gb_col, l)
        xp, attn_hi = _ffn(xp, *ffn2, l, attn=(qs, ks, vs, c1, c2, c3, half))
        for g, kvt in enumerate((kvt1, kvt2, kvt3)):
            pa[g].append(kvt)
        pb.append(nb[:, ZHALO - (B_CONV - 1):])
        pd.append(nd[:, GHALO - (D_CONV - 1):])

        attn_s = jnp.concatenate([attn_lo, attn_hi], axis=0).transpose(1, 0, 2)
        xs3, nb_s, vn_s, glu_s = _mixer_sample(
            xs.reshape(DEC_SEQ, DEC_BATCH, D_MODEL), attn_s, state_b_tm[l], state_d_tm[l], mix_w, wtab, btab, l)
        xs = _ffn(xs3.reshape(DEC_SEQ * DEC_BATCH, D_MODEL), *ffn2, l)
        k4 = ks.reshape(DEC_BATCH, DEC_SEQ, A_HEADS, HEAD_DIM)
        v4 = vs.reshape(DEC_BATCH, DEC_SEQ, A_HEADS, HEAD_DIM)
        for g in range(len(A_GROUPS)):
            hs = slice(g * A_HPG, (g + 1) * A_HPG)
            sa[g].append(jnp.stack([k4[:, :, hs], v4[:, :, hs]], axis=2))
        sb.append(nb_s.transpose(1, 0, 2))
        sc.append(vn_s.transpose(1, 0, 2))
        sd.append(jnp.concatenate([state_conv_d[l][:, DEC_SEQ:], glu_s.transpose(1, 0, 2)], axis=1))

    st = lambda xs_: jnp.stack(xs_, axis=0)
    new_kv = lambda ts: st(ts).reshape(DEPTH, BATCH, 2, A_HPG, HEAD_DIM, -1).transpose(0, 1, 5, 2, 3, 4)
    return (xp.reshape(BATCH, SEQ, D_MODEL),
            xs.reshape(DEC_SEQ, DEC_BATCH, D_MODEL).transpose(1, 0, 2),
            new_kv(pa[0]), new_kv(pa[1]), new_kv(pa[2]), st(sa[0]), st(sa[1]), st(sa[2]),
            st(pb), st(sb), st(sc), st(pd), st(sd))
```

```python
import functools

import numpy as np
import jax
import jax.numpy as jnp
from jax import lax
from jax.experimental import pallas as pl
from jax.experimental.pallas import tpu as pltpu

D_MODEL = 1024
BATCH = 4
SEQ = 4096
DEPTH = 2
DEC_BATCH = 128
DEC_SEQ = 8
HEAD_DIM = 64
A_GROUPS = ((128, 1), (512, 4), (2048, 16))
A_HPG = 4
A_HEADS = A_HPG * len(A_GROUPS)
A_WIDTH = A_HEADS * HEAD_DIM
A_OUT = A_HPG * HEAD_DIM
NK = 128
B_WIDTH = 512
B_CONV = 3
C_WIDTH = 512
C_CHUNK = 128
C_GROUPS = 4
C_GDIM = 128
D_WIDTH = 512
D_CONV = 31
D_FF = 2816
EPS = 1e-6
QKV_COLS = 3 * A_WIDTH
OFF_BGATE, OFF_CGATE, OFF_BIN, OFF_U, OFF_GV, OFF_GLUA, OFF_GLUG, OFF_GATES = (
    0, 512, 1024, 1536, 2048, 2560, 3072, 3584)
REST_COLS = OFF_GATES + 4 * D_MODEL

F32 = jnp.float32
BF16 = jnp.bfloat16
NEG = -1e30
SUBLANES = 8
LANES = 128
V7X_VMEM_LIMIT_BYTES = 56 * 1024 * 1024

FFN_TM = 512
FFN_FC = 256
QKV_TM = 1024
QKV_S_TM = 512
MIX_TM = 512
MIX_RC = 64
ZHALO = 8
GHALO = 32
ATT_TQ = 1024
SAMP_SB = 2
SAMP_SN = 64


def _cparams(*sem):
    return pltpu.CompilerParams(dimension_semantics=sem, vmem_limit_bytes=V7X_VMEM_LIMIT_BYTES)


def _resident(shape):
    nd = len(shape)
    return pl.BlockSpec(shape, lambda *_: (0,) * nd, pipeline_mode=pl.Buffered(1))


def _of_layer(shape, layer):
    nd = len(shape)
    return pl.BlockSpec((None,) + tuple(shape), lambda *_: (layer,) + (0,) * nd, pipeline_mode=pl.Buffered(1))


def _dot(a, b):
    return jnp.dot(a, b, preferred_element_type=F32)


def _dot_nt(a, b):
    return lax.dot_general(a, b, (((1,), (1,)), ((), ())), preferred_element_type=F32)


def _rms(x, g):
    return x * lax.rsqrt(jnp.mean(x * x, axis=-1, keepdims=True) + EPS) * g


def _ln(x, g, b):
    mu = jnp.mean(x, axis=-1, keepdims=True)
    xc = x - mu
    var = jnp.mean(xc * xc, axis=-1, keepdims=True)
    return xc * lax.rsqrt(var + EPS) * g + b


def _ffn_body(x_ref, pre_ref, post_ref, wg_ref, wu_ref, wd_ref, *rest, sb):
    if sb:
        q_ref, k_ref, v_ref, c1_ref, c2_ref, c3_ref, bias_ref, o_ref, attn_ref = rest
        caches = (c1_ref, c2_ref, c3_ref)
        staged = [_samp_scores(n, q_ref, k_ref, v_ref, caches, bias_ref) for n in range(sb)]
        staged = [_samp_probs(scores) + (new_v,) for scores, new_v in staged]
    else:
        (o_ref,) = rest
    x = x_ref[...]
    h = _rms(x, pre_ref[...]).astype(BF16)
    acc = jnp.zeros(x.shape, F32)
    chunks = D_FF // FFN_FC
    for c in range(chunks):
        if sb and c == chunks // 2:
            for n, (probs, den, new_v) in enumerate(staged):
                attn_ref[n] = _samp_out(n, probs, den, new_v, caches)
        sl = slice(c * FFN_FC, (c + 1) * FFN_FC)
        g = _dot(h, wg_ref[:, sl])
        u = _dot(h, wu_ref[:, sl])
        acc = acc + _dot((jax.nn.silu(g) * u).astype(BF16), wd_ref[sl, :])
    o_ref[...] = x + 0.5 * _rms(acc, post_ref[...])


def _ffn(x, pre_g, post_g, wg, wu, wd, layer, attn=None):
    rows = x.shape[0]
    tm = min(FFN_TM, rows)
    steps = rows // tm
    in_specs = [pl.BlockSpec((tm, D_MODEL), lambda i: (i, 0)),
                _of_layer((1, D_MODEL), layer), _of_layer((1, D_MODEL), layer),
                _of_layer((D_MODEL, D_FF), layer), _of_layer((D_MODEL, D_FF), layer),
                _of_layer((D_FF, D_MODEL), layer)]
    out_specs = [pl.BlockSpec((tm, D_MODEL), lambda i: (i, 0))]
    out_shape = [jax.ShapeDtypeStruct((rows, D_MODEL), F32)]
    operands = [x, pre_g, post_g, wg, wu, wd]
    sb = 0
    if attn is not None:
        *operands_a, first_seq = attn
        sb = SAMP_SB
        blk0 = first_seq // sb
        qspec = pl.BlockSpec((sb, DEC_SEQ, A_WIDTH), lambda i: (i + blk0, 0, 0))
        cspec = lambda n: pl.BlockSpec((1, sb, 2 * A_OUT, n), lambda i: (layer, i + blk0, 0, 0))
        in_specs += [qspec, qspec, qspec] + [cspec(n) for n in SAMP_CACHE_LEN] + [_resident((SAMP_Q, SAMP_KEYS))]
        out_specs.append(pl.BlockSpec((sb, DEC_SEQ, A_OUT), lambda i: (i, 0, 0)))
        out_shape.append(jax.ShapeDtypeStruct((steps * sb, DEC_SEQ, A_OUT), F32))
        operands += operands_a + [jnp.asarray(_sample_bias())]
    out = pl.pallas_call(
        functools.partial(_ffn_body, sb=sb),
        grid=(steps,),
        in_specs=in_specs,
        out_specs=out_specs,
        out_shape=out_shape,
        compiler_params=_cparams("parallel"),
        name="ffn_attn" if sb else "ffn",
    )(*operands)
    return out if sb else out[0]


Q_SCALE = HEAD_DIM ** -0.5


def _qkv_s_body(x_ref, g_ref, w_ref, q_ref, k_ref, v_ref):
    h = _rms(x_ref[...], g_ref[...]).astype(BF16)
    q_ref[...] = _dot(h, w_ref[:, 0:A_WIDTH]) * Q_SCALE
    k_ref[...] = _dot(h, w_ref[:, A_WIDTH:2 * A_WIDTH])
    v_ref[...] = _dot(h, w_ref[:, 2 * A_WIDTH:3 * A_WIDTH])


def _qkv_sample(x, g, w_qkv, layer):
    rows = x.shape[0]
    tm = min(QKV_S_TM, rows)
    out = jax.ShapeDtypeStruct((rows, A_WIDTH), F32)
    spec = pl.BlockSpec((tm, A_WIDTH), lambda i: (i, 0))
    return pl.pallas_call(
        _qkv_s_body,
        grid=(rows // tm,),
        in_specs=[pl.BlockSpec((tm, D_MODEL), lambda i: (i, 0)),
                  _of_layer((1, D_MODEL), layer), _of_layer((D_MODEL, QKV_COLS), layer)],
        out_specs=[spec, spec, spec],
        out_shape=[out, out, out],
        compiler_params=_cparams("parallel"),
        name="qkv_sample",
    )(x, g, w_qkv)


def _kv_tail(group, tm):
    keep = min(A_GROUPS[group][0], SEQ)
    rows = min(keep, tm)
    return (SEQ - keep) // tm, rows, tm - rows


def _qkv_p_body(x_ref, g_ref, w_ref, o1_ref, o2_ref, o3_ref, t1_ref, t2_ref, t3_ref, acc_ref, *, tm):
    h = _rms(x_ref[...], g_ref[...]).astype(BF16)
    for g, (_, dil) in enumerate(A_GROUPS):
        for part in range(3):
            src = part * A_WIDTH + g * A_OUT
            res = _dot(h, w_ref[:, src:src + A_OUT])
            for sub in range(A_OUT // LANES):
                acc_ref[src // LANES + sub] = res[:, sub * LANES:(sub + 1) * LANES]
        out = (o1_ref, o2_ref, o3_ref)[g]
        for c in range(3 * A_OUT // LANES):
            part, sub = divmod(c * LANES, A_OUT)
            src = (part * A_WIDTH + g * A_OUT + sub) // LANES
            for r in range(dil):
                val = acc_ref[src] if dil == 1 else acc_ref[src, pl.ds(r, tm // dil, stride=dil), :]
                if part == 0:
                    val = val * Q_SCALE
                out[0, r, :, c * LANES:(c + 1) * LANES] = val.astype(BF16)
    for g in range(len(A_GROUPS)):
        first, rows, row0 = _kv_tail(g, tm)
        out_t = (t1_ref, t2_ref, t3_ref)[g]

        @pl.when(pl.program_id(1) >= first)
        def _(g=g, rows=rows, row0=row0, out_t=out_t):
            for c in range(2 * A_OUT // LANES):
                part, sub = divmod(c * LANES, A_OUT)
                src = ((part + 1) * A_WIDTH + g * A_OUT + sub) // LANES
                out_t[0, c * LANES:(c + 1) * LANES, :] = acc_ref[src, row0:row0 + rows, :].T


def _qkv_prompt(x, g, w_qkv, layer):
    tm = QKV_TM
    steps = SEQ // tm
    dils = [dil for _, dil in A_GROUPS]
    tails = [_kv_tail(g, tm) for g in range(len(A_GROUPS))]
    tail_spec = lambda first, rows: pl.BlockSpec((1, 2 * A_OUT, rows), lambda b, j: (b, 0, jnp.maximum(j - first, 0)))
    return pl.pallas_call(
        functools.partial(_qkv_p_body, tm=tm),
        grid=(BATCH, steps),
        in_specs=[pl.BlockSpec((tm, D_MODEL), lambda b, j: (b * steps + j, 0)),
                  _of_layer((1, D_MODEL), layer), _of_layer((D_MODEL, QKV_COLS), layer)],
        out_specs=([pl.BlockSpec((1, dil, tm // dil, 3 * A_OUT), lambda b, j: (b, 0, j, 0)) for dil in dils]
                   + [tail_spec(first, rows) for first, rows, _ in tails]),
        out_shape=([jax.ShapeDtypeStruct((BATCH, dil, SEQ // dil, 3 * A_OUT), BF16) for dil in dils]
                   + [jax.ShapeDtypeStruct((BATCH, 2 * A_OUT, min(win, SEQ)), F32) for win, _ in A_GROUPS]),
        scratch_shapes=[pltpu.VMEM((QKV_COLS // LANES, tm, LANES), F32)],
        compiler_params=_cparams("parallel", "arbitrary"),
        name="qkv_prompt",
    )(x, g, w_qkv)


def _alibi_slopes():
    return np.exp2(-8.0 * np.arange(1, A_HEADS + 1, dtype=np.float64) / A_HEADS)


def _prompt_bias(group):
    _, dil = A_GROUPS[group]
    rel = NK + np.arange(NK)[:, None] - np.arange(2 * NK)[None, :]
    valid = (rel >= 0) & (rel <= NK)
    slopes = _alibi_slopes()[group * A_HPG:(group + 1) * A_HPG]
    bias = -slopes[:, None, None] * (dil * rel)[None].astype(np.float64)
    return np.where(valid[None], bias, NEG).astype(np.float32).reshape(A_HPG * NK, 2 * NK)


def _attn_p_body(q_ref, kp_ref, kc_ref, vp_ref, vc_ref, bias_ref, o_ref, lse_ref, *, tq, rb):
    first = (pl.program_id(2) == 0).astype(F32)
    lane_head = lax.broadcasted_iota(jnp.int32, (1, A_OUT), 1) // HEAD_DIM
    key_col = lax.broadcasted_iota(jnp.int32, (1, 2 * NK), 1)
    no_prev = jnp.where(key_col < NK, NEG, 0.0) * first
    blocks = [(r, j) for r in range(rb) for j in range(tq // NK)]
    blk = lambda ref, r, j: ref[0, r, j * NK:(j + 1) * NK, :]
    scores, values = [], []
    for r, j in blocks:
        qj = blk(q_ref, r, j)
        qs = jnp.concatenate([jnp.where(lane_head == h, qj, jnp.zeros_like(qj)) for h in range(A_HPG)], axis=0)
        k_prev, v_prev = ((kp_ref[0, r], vp_ref[0, r]) if j == 0
                          else (blk(kc_ref, r, j - 1), blk(vc_ref, r, j - 1)))
        s = _dot_nt(qs, jnp.concatenate([k_prev, blk(kc_ref, r, j)], axis=0)) + bias_ref[...]
        scores.append(s + no_prev if j == 0 else s)
        values.append(jnp.concatenate([v_prev, blk(vc_ref, r, j)], axis=0))
    stats = []
    for i in range(len(blocks)):
        m = jnp.max(scores[i], axis=-1, keepdims=True)
        p = jnp.exp(scores[i] - m)
        den = jnp.sum(p, axis=-1, keepdims=True)
        scores[i] = p.astype(BF16)
        stats.append((den, m + jnp.log(den)))
    for i, (r, j) in enumerate(blocks):
        den, lse = stats[i]
        pv = _dot(scores[i], values[i]) / den
        o_acc = jnp.zeros((NK, A_OUT), F32)
        l_acc = jnp.zeros((NK, A_OUT), F32)
        for h in range(A_HPG):
            rows = slice(h * NK, (h + 1) * NK)
            o_acc = jnp.where(lane_head == h, pv[rows], o_acc)
            l_acc = jnp.where(lane_head == h, lse[rows], l_acc)
        o_ref[0, r, j * NK:(j + 1) * NK, :] = o_acc
        lse_ref[0, r, j * NK:(j + 1) * NK, :] = l_acc


def _attn_prompt(qkv, group):
    _, dil = A_GROUPS[group]
    sub = SEQ // dil
    tq = min(ATT_TQ, sub)
    rb = min(dil, ATT_TQ // tq)
    cur = lambda col: pl.BlockSpec((1, rb, tq, A_OUT), lambda b, r, i: (b, r, i, col))
    prev = lambda col: pl.BlockSpec((1, rb, NK, A_OUT),
                                    lambda b, r, i: (b, r, jnp.maximum(i * (tq // NK) - 1, 0), col))
    out = jax.ShapeDtypeStruct((BATCH, dil, sub, A_OUT), F32)
    return pl.pallas_call(
        functools.partial(_attn_p_body, tq=tq, rb=rb),
        grid=(BATCH, dil // rb, sub // tq),
        in_specs=[cur(0), prev(1), cur(1), prev(2), cur(2), _resident((A_HPG * NK, 2 * NK))],
        out_specs=[cur(0), cur(0)],
        out_shape=[out, out],
        compiler_params=_cparams("parallel", "parallel", "arbitrary"),
        name=f"attn_prompt_g{group}",
    )(qkv, qkv, qkv, qkv, qkv, jnp.asarray(_prompt_bias(group)))


SAMP_CACHE_LEN = tuple(win for win, _ in A_GROUPS)
SAMP_NEW_SLOT = 128
SAMP_SEG = []
_off = 0
for _len in SAMP_CACHE_LEN:
    SAMP_SEG.append((_off, _off + _len, _off + _len + SAMP_NEW_SLOT))
    _off += _len + SAMP_NEW_SLOT
SAMP_KEYS = _off
SAMP_Q = A_HPG * DEC_SEQ


def _sample_bias():
    slopes = _alibi_slopes()
    bias = np.full((SAMP_Q, SAMP_KEYS), NEG, np.float64)
    for c in range(SAMP_Q):
        h, t = divmod(c, DEC_SEQ)
        for g, (win, dil) in enumerate(A_GROUPS):
            c0, n0, _ = SAMP_SEG[g]
            sl = slopes[g * A_HPG + h]
            dist = win + t - np.arange(win)
            ok = (dist % dil == 0) & (dist <= win)
            bias[c, c0:c0 + win] = np.where(ok, -sl * dist, NEG)
            for t2 in range(t + 1):
                if (t - t2) % dil == 0:
                    bias[c, n0 + t2] = -sl * (t - t2)
    return bias.astype(np.float32)


def _samp_scores(n, q_ref, k_ref, v_ref, caches, bias_ref):
    lane_head = lax.broadcasted_iota(jnp.int32, (1, A_OUT), 1) // HEAD_DIM
    qn, kn, vn = q_ref[n], k_ref[n], v_ref[n]
    pad = jnp.zeros((SAMP_NEW_SLOT - DEC_SEQ, A_OUT), F32)
    scores, new_v = [], []
    for g in range(len(A_GROUPS)):
        c0, n0, e0 = SAMP_SEG[g]
        cols = slice(g * A_OUT, (g + 1) * A_OUT)
        k_new = jnp.concatenate([kn[:, cols], pad], axis=0).astype(BF16)
        new_v.append(jnp.concatenate([vn[:, cols], pad], axis=0).astype(BF16))
        qbd = jnp.concatenate([jnp.where(lane_head == h, qn[:, cols], 0.0) for h in range(A_HPG)],
                              axis=0).astype(BF16)
        k_t = caches[g][0, n, 0:A_OUT, :].astype(BF16)
        scores.append(_dot(qbd, k_t) + bias_ref[:, c0:n0])
        scores.append(_dot_nt(qbd, k_new) + bias_ref[:, n0:e0])
    return scores, new_v


def _samp_probs(scores):
    m = functools.reduce(jnp.maximum, [jnp.max(s, axis=-1, keepdims=True) for s in scores])
    probs = [jnp.exp(s - m) for s in scores]
    den = functools.reduce(jnp.add, [jnp.sum(p, axis=-1, keepdims=True) for p in probs])
    return [p.astype(BF16) for p in probs], den


def _samp_out(n, probs, den, new_v, caches):
    lane_head = lax.broadcasted_iota(jnp.int32, (1, A_OUT), 1) // HEAD_DIM
    out = jnp.zeros((SAMP_Q, A_OUT), F32)
    for g in range(len(A_GROUPS)):
        v_t = caches[g][0, n, A_OUT:2 * A_OUT, :].astype(BF16)
        out = out + _dot_nt(probs[2 * g], v_t) + _dot(probs[2 * g + 1], new_v[g])
    out = out / den
    attn = jnp.zeros((DEC_SEQ, A_OUT), F32)
    for h in range(A_HPG):
        attn = jnp.where(lane_head == h, out[h * DEC_SEQ:(h + 1) * DEC_SEQ, :], attn)
    return attn


def _proj(h, w_ref, off, width):
    return _dot(h, w_ref[0, :, off:off + width])


def _gate(h, w_ref, branch):
    return jax.nn.sigmoid(_proj(h, w_ref, OFF_GATES + branch * D_MODEL, D_MODEL))


_MIX_WEIGHT_SHAPES = (
    (1, D_MODEL), (1, D_MODEL),
    (D_MODEL, REST_COLS),
    (A_OUT, D_MODEL), (B_WIDTH, D_MODEL), (C_WIDTH, D_MODEL), (D_WIDTH, D_MODEL), (D_MODEL, D_MODEL),
    (B_CONV, B_WIDTH), (1, C_WIDTH), (1, C_WIDTH),
    (D_CONV, D_WIDTH), (1, D_WIDTH), (1, D_WIDTH), (1, D_WIDTH),
)
_MIX_W_IN = 2


def _mix_weight_specs(layer):
    specs = [_of_layer(s, layer) for s in _MIX_WEIGHT_SHAPES]
    specs[_MIX_W_IN] = pl.BlockSpec((pl.Element(1), pl.Element(D_MODEL), pl.Element(REST_COLS)),
                                    lambda *_: (layer, 0, QKV_COLS), pipeline_mode=pl.Buffered(1))
    return specs


def _mixer_p_body(x_ref, o1_ref, o2_ref, o3_ref, l1_ref, l2_ref, l3_ref,
                  pre_ref, post_ref, w_ref, woa_ref, wob_ref, woc_ref, wod_ref, wo_ref,
                  cbw_ref, glng_ref, glnb_ref, cdw_ref, cdb_ref, dlng_ref, dlnb_ref,
                  gws_ref, gbc_ref,
                  xo_ref, nb_ref, nd_ref, zbuf, gbuf, nat, *, tm):
    @pl.when(pl.program_id(1) == 0)
    def _():
        zbuf[0:ZHALO, :] = jnp.zeros((ZHALO, B_WIDTH), F32)
        gbuf[0:GHALO, :] = jnp.zeros((GHALO, D_WIDTH), F32)

    x = x_ref[...]
    h = _rms(x, pre_ref[...]).astype(BF16)

    gbuf[GHALO:GHALO + tm, :] = (_proj(h, w_ref, OFF_GLUA, D_WIDTH)
                                 * jax.nn.sigmoid(_proj(h, w_ref, OFF_GLUG, D_WIDTH)))
    wrows = MIX_RC + GHALO
    dc = []
    for ci in range(tm // MIX_RC):
        win = gbuf[ci * MIX_RC:ci * MIX_RC + wrows, :]
        phase = [win] + [pltpu.roll(win, wrows - s, axis=0) for s in range(1, SUBLANES)]
        acc = jnp.broadcast_to(cdb_ref[...], (MIX_RC, D_WIDTH))
        for k in range(D_CONV):
            a, s = divmod(GHALO - (D_CONV - 1) + k, SUBLANES)
            acc = acc + cdw_ref[k:k + 1, :] * phase[s][a * SUBLANES:a * SUBLANES + MIX_RC, :]
        dc.append(acc)
    tail = gbuf[tm:tm + GHALO, :]
    nd_ref[0] = tail
    gbuf[0:GHALO, :] = tail

    def natural(ref, slot, dil):
        halves = A_OUT // LANES
        for r in range(dil):
            for c in range(halves):
                nat[slot * halves + c, pl.ds(r, tm // dil, stride=dil), :] = ref[0, r, :, c * LANES:(c + 1) * LANES]
        return jnp.concatenate([nat[slot * halves + c] for c in range(halves)], axis=1)

    dil2, dil3 = A_GROUPS[1][1], A_GROUPS[2][1]
    o1, l1 = o1_ref[...], l1_ref[...]
    o2, l2 = natural(o2_ref, 0, dil2), natural(l2_ref, 1, dil2)
    o3, l3 = natural(o3_ref, 2, dil3), natural(l3_ref, 3, dil3)
    mx = jnp.maximum(jnp.maximum(l1, l2), l3)
    e1, e2, e3 = jnp.exp(l1 - mx), jnp.exp(l2 - mx), jnp.exp(l3 - mx)
    attn = (e1 * o1 + e2 * o2 + e3 * o3) / (e1 + e2 + e3)


    zbuf[ZHALO:ZHALO + tm, :] = _proj(h, w_ref, OFF_CGATE, B_WIDTH) * _proj(h, w_ref, OFF_BIN, B_WIDTH)
    b_gate = _proj(h, w_ref, OFF_BGATE, B_WIDTH)
    merged = _gate(h, w_ref, 0) * _dot(attn.astype(BF16), woa_ref[...])
    conv = jnp.zeros((tm, B_WIDTH), F32)
    for k in range(B_CONV):
        start = ZHALO - (B_CONV - 1) + k
        conv = conv + cbw_ref[k:k + 1, :] * zbuf[start:start + tm, :]
    tail = zbuf[tm:tm + ZHALO, :]
    nb_ref[0] = tail
    zbuf[0:ZHALO, :] = tail

    vn = _ln(_proj(h, w_ref, OFF_GV, C_WIDTH), glng_ref[...], glnb_ref[...]).astype(BF16)
    u = _proj(h, w_ref, OFF_U, C_WIDTH)
    gate1 = _gate(h, w_ref, 1)
    merged = merged + gate1 * _dot((b_gate * conv).astype(BF16), wob_ref[...])
    gate2 = _gate(h, w_ref, 2)
    r_i = lax.broadcasted_iota(jnp.int32, (C_CHUNK, C_CHUNK), 0)
    c_i = lax.broadcasted_iota(jnp.int32, (C_CHUNK, C_CHUNK), 1)
    ws = [jnp.where(r_i >= c_i, gws_ref[g], 0.0).astype(BF16) for g in range(C_GROUPS)]
    chunks = []
    for c in range(tm // C_CHUNK):
        rows = slice(c * C_CHUNK, (c + 1) * C_CHUNK)
        chunks.append(jnp.concatenate(
            [_dot(ws[g], vn[rows, g * C_GDIM:(g + 1) * C_GDIM]) + gbc_ref[:, g:g + 1]
             for g in range(C_GROUPS)], axis=1))
    mixed = jnp.concatenate(chunks, axis=0)
    gate3 = _gate(h, w_ref, 3)
    merged = merged + gate2 * _dot((u * mixed).astype(BF16), woc_ref[...])

    dn = _ln(jnp.concatenate(dc, axis=0), dlng_ref[...], dlnb_ref[...])
    merged = merged + gate3 * _dot(jax.nn.silu(dn).astype(BF16), wod_ref[...])

    y = _dot(merged.astype(BF16), wo_ref[...])
    xo_ref[...] = x + _rms(y, post_ref[...])


def _mixer_prompt(x, o, lse, weights, gws, gb_col, layer):
    tm = MIX_TM
    steps = SEQ // tm
    row = lambda width: pl.BlockSpec((tm, width), lambda b, j: (b * steps + j, 0))
    tail = lambda n: pl.BlockSpec((1, n, B_WIDTH), lambda b, j: (b, 0, 0))
    res = lambda dil: pl.BlockSpec((1, dil, tm // dil, A_OUT), lambda b, j: (b, 0, j, 0))
    attn_specs = [row(A_OUT)] + [res(dil) for _, dil in A_GROUPS[1:]]
    flat = lambda a: a.reshape(BATCH * SEQ, A_OUT)
    return pl.pallas_call(
        functools.partial(_mixer_p_body, tm=tm),
        grid=(BATCH, steps),
        in_specs=([row(D_MODEL)] + attn_specs + attn_specs
                  + _mix_weight_specs(layer)
                  + [_of_layer((C_GROUPS, C_CHUNK, C_CHUNK), layer), _of_layer((C_CHUNK, C_GROUPS), layer)]),
        out_specs=[row(D_MODEL), tail(ZHALO), tail(GHALO)],
        out_shape=[jax.ShapeDtypeStruct((BATCH * SEQ, D_MODEL), F32),
                   jax.ShapeDtypeStruct((BATCH, ZHALO, B_WIDTH), F32),
                   jax.ShapeDtypeStruct((BATCH, GHALO, D_WIDTH), F32)],
        scratch_shapes=[pltpu.VMEM((tm + ZHALO, B_WIDTH), F32),
                        pltpu.VMEM((tm + GHALO, D_WIDTH), F32),
                        pltpu.VMEM((4 * A_OUT // LANES, tm, LANES), F32)],
        compiler_params=_cparams("parallel", "arbitrary"),
        name="mixer_prompt",
    )(x, flat(o[0]), o[1], o[2], flat(lse[0]), lse[1], lse[2], *weights, gws, gb_col)


def _mixer_s_body(x_ref, attn_ref, sb_ref, sd_ref,
                  pre_ref, post_ref, w_ref, woa_ref, wob_ref, woc_ref, wod_ref, wo_ref,
                  cbw_ref, glng_ref, glnb_ref, cdw_ref, cdb_ref, dlng_ref, dlnb_ref,
                  wtab_ref, btab_ref,
                  xo_ref, nb_ref, vn_ref, glu_ref, *, sn):
    rows = DEC_SEQ * sn
    x = x_ref[...].reshape(rows, D_MODEL)
    h = _rms(x, pre_ref[...]).astype(BF16)
    slab = lambda a, t: a[t * sn:(t + 1) * sn, :]

    merged = _gate(h, w_ref, 0) * _dot(attn_ref[...].reshape(rows, A_OUT).astype(BF16), woa_ref[...])

    zc = _proj(h, w_ref, OFF_CGATE, B_WIDTH) * _proj(h, w_ref, OFF_BIN, B_WIDTH)
    zpad = [sb_ref[i] for i in range(B_CONV - 1)] + [slab(zc, t) for t in range(DEC_SEQ)]
    conv = jnp.concatenate(
        [sum(cbw_ref[k:k + 1, :] * zpad[t + k] for k in range(B_CONV)) for t in range(DEC_SEQ)], axis=0)
    br = _dot((_proj(h, w_ref, OFF_BGATE, B_WIDTH) * conv).astype(BF16), wob_ref[...])
    merged = merged + _gate(h, w_ref, 1) * br
    for i in range(B_CONV - 1):
        nb_ref[i] = zpad[DEC_SEQ + i]

    vn = _ln(_proj(h, w_ref, OFF_GV, C_WIDTH), glng_ref[...], glnb_ref[...])
    vn_ref[...] = vn.reshape(DEC_SEQ, sn, C_WIDTH)
    mixed = []
    for t in range(DEC_SEQ):
        acc = jnp.broadcast_to(btab_ref[t:t + 1, :], (sn, C_WIDTH))
        for s in range(t + 1):
            acc = acc + wtab_ref[t * DEC_SEQ + s:t * DEC_SEQ + s + 1, :] * slab(vn, s)
        mixed.append(acc)
    mixed = jnp.concatenate(mixed, axis=0)
    br = _dot((_proj(h, w_ref, OFF_U, C_WIDTH) * mixed).astype(BF16), woc_ref[...])
    merged = merged + _gate(h, w_ref, 2) * br

    glu = _proj(h, w_ref, OFF_GLUA, D_WIDTH) * jax.nn.sigmoid(_proj(h, w_ref, OFF_GLUG, D_WIDTH))
    glu_ref[...] = glu.reshape(DEC_SEQ, sn, D_WIDTH)
    gpad = [sd_ref[i] for i in range(D_CONV - 1)] + [slab(glu, t) for t in range(DEC_SEQ)]
    dc = []
    for t in range(DEC_SEQ):
        acc = jnp.broadcast_to(cdb_ref[...], (sn, D_WIDTH))
        for k in range(D_CONV):
            acc = acc + cdw_ref[k:k + 1, :] * gpad[t + k]
        dc.append(acc)
    dn = _ln(jnp.concatenate(dc, axis=0), dlng_ref[...], dlnb_ref[...])
    br = _dot(jax.nn.silu(dn).astype(BF16), wod_ref[...])
    merged = merged + _gate(h, w_ref, 3) * br

    y = _dot(merged.astype(BF16), wo_ref[...])
    xo_ref[...] = (x + _rms(y, post_ref[...])).reshape(DEC_SEQ, sn, D_MODEL)


def _mixer_sample(x, attn, state_b, state_d, weights, wtab, btab, layer):
    sn = SAMP_SN
    tslab = lambda n, width: pl.BlockSpec((n, sn, width), lambda i: (0, i, 0))
    return pl.pallas_call(
        functools.partial(_mixer_s_body, sn=sn),
        grid=(DEC_BATCH // sn,),
        in_specs=([tslab(DEC_SEQ, D_MODEL), tslab(DEC_SEQ, A_OUT),
                   tslab(B_CONV - 1, B_WIDTH), tslab(D_CONV - 1, D_WIDTH)]
                  + _mix_weight_specs(layer)
                  + [_of_layer((DEC_SEQ * DEC_SEQ, C_WIDTH), layer), _of_layer((DEC_SEQ, C_WIDTH), layer)]),
        out_specs=[tslab(DEC_SEQ, D_MODEL), tslab(B_CONV - 1, B_WIDTH),
                   tslab(DEC_SEQ, C_WIDTH), tslab(DEC_SEQ, D_WIDTH)],
        out_shape=[jax.ShapeDtypeStruct((DEC_SEQ, DEC_BATCH, D_MODEL), F32),
                   jax.ShapeDtypeStruct((B_CONV - 1, DEC_BATCH, B_WIDTH), F32),
                   jax.ShapeDtypeStruct((DEC_SEQ, DEC_BATCH, C_WIDTH), F32),
                   jax.ShapeDtypeStruct((DEC_SEQ, DEC_BATCH, D_WIDTH), F32)],
        compiler_params=_cparams("parallel"),
        name="mixer_sample",
    )(x, attn, state_b, state_d, *weights, wtab, btab)


def kernel(x_prompt, x_sample, cache_attn_w128, cache_attn_w512, cache_attn_w2048, state_conv_b, state_conv_d, ffn1_pre_g, ffn1_post_g, ffn1_w_gate, ffn1_w_up, ffn1_w_down, mix_pre_g, mix_post_g, w_in, w_out_a, conv_b_w, w_out_b, gmlp_ln_g, gmlp_ln_b, gmlp_ws, gmlp_b, w_out_c, conv_d_w, conv_d_b, conv_d_ln_g, conv_d_ln_b, w_out_d, w_o, ffn2_pre_g, ffn2_post_g, ffn2_w_gate, ffn2_w_up, ffn2_w_down):
    bf = lambda a: a.astype(BF16)
    vec = lambda a: a.reshape(DEPTH, 1, -1)

    xp = x_prompt.reshape(BATCH * SEQ, D_MODEL)
    xs = x_sample.transpose(1, 0, 2).reshape(DEC_SEQ * DEC_BATCH, D_MODEL)
    time_minor = lambda c: c.transpose(0, 1, 3, 4, 5, 2).reshape(DEPTH, DEC_BATCH, 2 * A_OUT, c.shape[2])
    c1, c2, c3 = time_minor(cache_attn_w128), time_minor(cache_attn_w512), time_minor(cache_attn_w2048)
    state_b_tm = state_conv_b.transpose(0, 2, 1, 3)
    state_d_tm = state_conv_d.transpose(0, 2, 1, 3)

    ffn1 = (vec(ffn1_pre_g), vec(ffn1_post_g), bf(ffn1_w_gate), bf(ffn1_w_up), bf(ffn1_w_down))
    ffn2 = (vec(ffn2_pre_g), vec(ffn2_post_g), bf(ffn2_w_gate), bf(ffn2_w_up), bf(ffn2_w_down))
    w_in_bf = bf(w_in)
    pre = vec(mix_pre_g)
    mix_w = (pre, vec(mix_post_g), w_in_bf,
             bf(w_out_a), bf(w_out_b), bf(w_out_c), bf(w_out_d), bf(w_o),
             conv_b_w, vec(gmlp_ln_g), vec(gmlp_ln_b),
             conv_d_w, vec(conv_d_b), vec(conv_d_ln_g), vec(conv_d_ln_b))
    gb_col = gmlp_b.transpose(0, 2, 1)
    ws8 = gmlp_ws[:, :, :DEC_SEQ, :DEC_SEQ].transpose(0, 2, 3, 1).reshape(DEPTH, DEC_SEQ * DEC_SEQ, C_GROUPS)
    wtab = jnp.repeat(ws8, C_GDIM, axis=2)
    btab = jnp.repeat(gb_col[:, :DEC_SEQ], C_GDIM, axis=2)

    pa, sa = [[] for _ in A_GROUPS], [[] for _ in A_GROUPS]
    pb, sb, sc, pd, sd = [], [], [], [], []
    for l in range(DEPTH):
        xs = _ffn(xs, *ffn1, l)
        qs, ks, vs = _qkv_sample(xs, pre, w_in_bf, l)
        seq_major = lambda a: a.reshape(DEC_SEQ, DEC_BATCH, A_WIDTH).transpose(1, 0, 2)
        qs, ks, vs = seq_major(qs), seq_major(ks), seq_major(vs)
        half = DEC_BATCH // 2

        xp, attn_lo = _ffn(xp, *ffn1, l, attn=(qs, ks, vs, c1, c2, c3, 0))
        *qkv, kvt1, kvt2, kvt3 = _qkv_prompt(xp, pre, w_in_bf, l)
        outs = [_attn_prompt(qkv[g], g) for g in range(len(A_GROUPS))]
        xp, nb, nd = _mixer_prompt(xp, [o for o, _ in outs], [s for _, s in outs], mix_w, gmlp_ws, gb_col, l)
        xp, attn_hi = _ffn(xp, *ffn2, l, attn=(qs, ks, vs, c1, c2, c3, half))
        for g, kvt in enumerate((kvt1, kvt2, kvt3)):
            pa[g].append(kvt)
        pb.append(nb[:, ZHALO - (B_CONV - 1):])
        pd.append(nd[:, GHALO - (D_CONV - 1):])

        attn_s = jnp.concatenate([attn_lo, attn_hi], axis=0).transpose(1, 0, 2)
        xs3, nb_s, vn_s, glu_s = _mixer_sample(
            xs.reshape(DEC_SEQ, DEC_BATCH, D_MODEL), attn_s, state_b_tm[l], state_d_tm[l], mix_w, wtab, btab, l)
        xs = _ffn(xs3.reshape(DEC_SEQ * DEC_BATCH, D_MODEL), *ffn2, l)
        k4 = ks.reshape(DEC_BATCH, DEC_SEQ, A_HEADS, HEAD_DIM)
        v4 = vs.reshape(DEC_BATCH, DEC_SEQ, A_HEADS, HEAD_DIM)
        for g in range(len(A_GROUPS)):
            hs = slice(g * A_HPG, (g + 1) * A_HPG)
            sa[g].append(jnp.stack([k4[:, :, hs], v4[:, :, hs]], axis=2))
        sb.append(nb_s.transpose(1, 0, 2))
        sc.append(vn_s.transpose(1, 0, 2))
        sd.append(jnp.concatenate([state_conv_d[l][:, DEC_SEQ:], glu_s.transpose(1, 0, 2)], axis=1))

    st = lambda xs_: jnp.stack(xs_, axis=0)
    new_kv = lambda ts: st(ts).reshape(DEPTH, BATCH, 2, A_HPG, HEAD_DIM, -1).transpose(0, 1, 5, 2, 3, 4)
    return (xp.reshape(BATCH, SEQ, D_MODEL),
            xs.reshape(DEC_SEQ, DEC_BATCH, D_MODEL).transpose(1, 0, 2),
            new_kv(pa[0]), new_kv(pa[1]), new_kv(pa[2]), st(sa[0]), st(sa[1]), st(sa[2]),
            st(pb), st(sb), st(sc), st(pd), st(sd))
```
